```python
import jax
import jax.numpy as jnp
from jax import lax
import numpy as np


D_MODEL = 1024
BATCH = 32
SEQ = 2048
DEPTH = 4

MEM_LEN = 256
NORM_EPS = 1e-6
D_FF = 4 * D_MODEL
MASK_VALUE = -1e30
LOG_FLOOR = 1e-30

FOX_HEADS = 8
FOX_HEAD_DIM = 64
FOX_WIDTH = FOX_HEADS * FOX_HEAD_DIM
FOX_BLOCK = 128
FOX_FGATE_BIAS_CENTER = 2.0

HGRN_HEADS = 4
HGRN_KEY_DIM = 128
HGRN_VAL_DIM = 128
HGRN_KEY_WIDTH = HGRN_HEADS * HGRN_KEY_DIM
HGRN_VAL_WIDTH = HGRN_HEADS * HGRN_VAL_DIM
HGRN_CHUNK = 64

EVEN_PROJ = 3 * FOX_WIDTH + FOX_HEADS + 2 * HGRN_KEY_WIDTH + 2 * HGRN_VAL_WIDTH
EVEN_MIX_WIDTH = FOX_WIDTH + HGRN_VAL_WIDTH

SSM_INNER = 2 * D_MODEL
SSM_HEAD_DIM = 64
SSM_HEADS = SSM_INNER // SSM_HEAD_DIM
SSM_GROUPS = 8
SSM_HEADS_PER_GROUP = SSM_HEADS // SSM_GROUPS
SSM_STATE = 128
SSM_CONV = 4
SSM_CHUNK = 128
SSM_CONV_CH = SSM_INNER + 2 * SSM_GROUPS * SSM_STATE
SSM_PROJ = SSM_INNER + SSM_CONV_CH + SSM_HEADS
SSM_DT_MIN = 1e-3
SSM_DT_MAX = 1e-1

XATTN_HEADS = 4
XATTN_HEAD_DIM = 128
XATTN_WIDTH = XATTN_HEADS * XATTN_HEAD_DIM

N_EVEN = (DEPTH + 1) // 2
N_ODD = DEPTH // 2

kernel_name = 'hybrid_fox_hgrn2_mamba2_xattn_block'


def rmsnorm(x, gain):
    xf = x.astype(jnp.float32)
    y = xf * lax.rsqrt(jnp.mean(xf * xf, axis=-1, keepdims=True) + NORM_EPS)
    return (y * gain.astype(jnp.float32)).astype(x.dtype)


def group_rmsnorm(x, gain, groups):
    shp = x.shape
    xf = x.astype(jnp.float32).reshape(shp[:-1] + (groups, shp[-1] // groups))
    y = xf * lax.rsqrt(jnp.mean(xf * xf, axis=-1, keepdims=True) + NORM_EPS)
    return (y.reshape(shp) * gain.astype(jnp.float32)).astype(x.dtype)


def split_cols(t, sizes):
    offs = np.cumsum(sizes)[:-1].tolist()
    return jnp.split(t, offs, axis=-1)


def masked_exp(logd, mask):
    return jnp.where(mask, jnp.exp(jnp.where(mask, logd, 0.0)), 0.0)


def forgetting_attention(q, k, v, log_f):
    bsz, seq, heads, hd = q.shape
    c = jnp.cumsum(log_f, axis=1).transpose(0, 2, 1)
    scale = hd ** -0.5
    outs = []
    for blk in range(seq // FOX_BLOCK):
        q0 = blk * FOX_BLOCK
        q1 = q0 + FOX_BLOCK
        s = jnp.einsum('bqhd,bkhd->bhqk', q[:, q0:q1], k[:, :q1]).astype(jnp.float32) * scale
        s = s + c[:, :, q0:q1, None] - c[:, :, None, :q1]
        mask = (q0 + jnp.arange(FOX_BLOCK))[:, None] >= jnp.arange(q1)[None, :]
        s = jnp.where(mask, s, MASK_VALUE)
        p = jax.nn.softmax(s, axis=-1).astype(v.dtype)
        outs.append(jnp.einsum('bhqk,bkhd->bqhd', p, v[:, :q1]))
    return jnp.concatenate(outs, axis=1)


def hgrn2_recurrence(q, k, v, log_f):
    bsz, seq, heads, dk = q.shape
    dv = v.shape[-1]
    n_chunks = seq // HGRN_CHUNK

    def chunks(t):
        t = t.astype(jnp.float32).reshape(bsz, n_chunks, HGRN_CHUNK, heads, t.shape[-1])
        return t.transpose(1, 0, 3, 2, 4)

    causal = jnp.tril(jnp.ones((HGRN_CHUNK, HGRN_CHUNK), dtype=bool))[:, :, None]

    def step(state, inp):
        qc, kc, vc, gc = inp
        b = jnp.cumsum(gc, axis=2)
        diff = b[:, :, :, None, :] - b[:, :, None, :, :]
        decay = masked_exp(diff, causal)
        scores = jnp.einsum('bhtk,bhtsk,bhsk->bhts', qc, decay, kc)
        out = (jnp.einsum('bhts,bhsv->bhtv', scores, vc)
               + jnp.einsum('bhtk,bhkv->bhtv', qc * jnp.exp(b), state))
        b_last = b[:, :, -1:, :]
        state = (state * jnp.exp(b_last[:, :, 0, :, None])
                 + jnp.einsum('bhsk,bhsv->bhkv', kc * jnp.exp(b_last - b), vc))
        return state, out

    state0 = jnp.zeros((bsz, heads, dk, dv), jnp.float32)
    _, out = lax.scan(step, state0, (chunks(q), chunks(k), chunks(v), chunks(log_f)))
    return out.transpose(1, 0, 3, 2, 4).reshape(bsz, seq, heads, dv).astype(v.dtype)


def ssd_chunked_scan(x, dt, a, bm, cm):
    bsz, seq, heads, hd = x.shape
    n_chunks = seq // SSM_CHUNK

    def chunks(t):
        t = t.astype(jnp.float32).reshape((bsz, n_chunks, SSM_CHUNK) + t.shape[2:])
        return jnp.moveaxis(t, 1, 0)

    causal = jnp.tril(jnp.ones((SSM_CHUNK, SSM_CHUNK), dtype=bool))[None, :, :, None]

    def step(state, inp):
        xc, dtc, bc, cc = inp
        cum = jnp.cumsum(dtc * a, axis=1)
        seg = cum[:, :, None, :] - cum[:, None, :, :]
        decay = masked_exp(seg, causal)
        cb = jnp.repeat(jnp.einsum('btgn,bsgn->btsg', cc, bc), SSM_HEADS_PER_GROUP, axis=-1)
        y = jnp.einsum('btsh,bsh,bshp->bthp', cb * decay, dtc, xc)
        ch = jnp.repeat(cc, SSM_HEADS_PER_GROUP, axis=2)
        bh = jnp.repeat(bc, SSM_HEADS_PER_GROUP, axis=2)
        y = y + jnp.einsum('bthn,bhpn->bthp', ch, state) * jnp.exp(cum)[..., None]
        w = jnp.exp(cum[:, -1:, :] - cum) * dtc
        state = (state * jnp.exp(cum[:, -1, :])[:, :, None, None]
                 + jnp.einsum('bsh,bshp,bshn->bhpn', w, xc, bh))
        return state, y

    state0 = jnp.zeros((bsz, heads, hd, SSM_STATE), jnp.float32)
    _, y = lax.scan(step, state0, (chunks(x), chunks(dt), chunks(bm), chunks(cm)))
    return jnp.moveaxis(y, 0, 1).reshape(bsz, seq, heads, hd).astype(x.dtype)


def fox_hgrn2_mixer(hn, w_in, fgate_bias, lb, out_norm_gain, w_out):
    bsz, seq, _ = hn.shape
    fq, fk, fv, fg, hq, hf, hi, hg = split_cols(
        hn @ w_in,
        (FOX_WIDTH, FOX_WIDTH, FOX_WIDTH, FOX_HEADS,
         HGRN_KEY_WIDTH, HGRN_KEY_WIDTH, HGRN_VAL_WIDTH, HGRN_VAL_WIDTH))
    log_fox_f = jax.nn.log_sigmoid(fg.astype(jnp.float32) + fgate_bias.astype(jnp.float32))
    a_out = forgetting_attention(
        fq.reshape(bsz, seq, FOX_HEADS, FOX_HEAD_DIM),
        fk.reshape(bsz, seq, FOX_HEADS, FOX_HEAD_DIM),
        fv.reshape(bsz, seq, FOX_HEADS, FOX_HEAD_DIM),
        log_fox_f).reshape(bsz, seq, FOX_WIDTH)
    hf32 = hf.astype(jnp.float32)
    lb = lb.astype(jnp.float32)
    log_lb = jnp.log(jnp.maximum(lb, LOG_FLOOR))
    log_gate = jnp.log1p(-lb) + jax.nn.log_sigmoid(hf32)
    m = jnp.maximum(log_lb, log_gate)
    log_f = m + jnp.log1p(jnp.exp(-jnp.abs(log_lb - log_gate)))
    k_in = (1.0 - lb) * jax.nn.sigmoid(-hf32)
    q_h = jax.nn.silu(hq)
    b_out = hgrn2_recurrence(
        q_h.reshape(bsz, seq, HGRN_HEADS, HGRN_KEY_DIM),
        k_in.reshape(bsz, seq, HGRN_HEADS, HGRN_KEY_DIM),
        hi.reshape(bsz, seq, HGRN_HEADS, HGRN_VAL_DIM),
        log_f.reshape(bsz, seq, HGRN_HEADS, HGRN_KEY_DIM)).reshape(bsz, seq, HGRN_VAL_WIDTH)
    b_out = group_rmsnorm(b_out, out_norm_gain, HGRN_HEADS) * jax.nn.silu(hg)
    mixed = jnp.concatenate([a_out, b_out.astype(a_out.dtype)], axis=-1)
    return mixed @ w_out


def mamba2_mixer(hn, w_in, conv_w, conv_b, dt_bias, a_log, d_skip, norm_gain, w_out):
    bsz, seq, _ = hn.shape
    z, xbc, dt_raw = split_cols(hn @ w_in, (SSM_INNER, SSM_CONV_CH, SSM_HEADS))
    xbc = lax.conv_general_dilated(
        xbc, conv_w[:, None, :], window_strides=(1,), padding=[(SSM_CONV - 1, 0)],
        dimension_numbers=('NWC', 'WIO', 'NWC'), feature_group_count=SSM_CONV_CH)
    xbc = jax.nn.silu(xbc + conv_b)
    xs, bm, cm = split_cols(xbc, (SSM_INNER, SSM_GROUPS * SSM_STATE, SSM_GROUPS * SSM_STATE))
    xs = xs.reshape(bsz, seq, SSM_HEADS, SSM_HEAD_DIM)
    bm = bm.reshape(bsz, seq, SSM_GROUPS, SSM_STATE)
    cm = cm.reshape(bsz, seq, SSM_GROUPS, SSM_STATE)
    dt = jax.nn.softplus(dt_raw.astype(jnp.float32) + dt_bias.astype(jnp.float32))
    a = -jnp.exp(a_log.astype(jnp.float32))
    y = ssd_chunked_scan(xs, dt, a, bm, cm)
    y = y + (d_skip[:, None] * xs).astype(y.dtype)
    y = y.reshape(bsz, seq, SSM_INNER) * jax.nn.silu(z)
    y = group_rmsnorm(y, norm_gain, SSM_GROUPS)
    return y @ w_out


def memory_cross_attention(hn, mem_n, w_q, w_kv, w_o):
    bsz, seq, _ = hn.shape
    q = (hn @ w_q).reshape(bsz, seq, XATTN_HEADS, XATTN_HEAD_DIM)
    k, v = split_cols(mem_n @ w_kv, (XATTN_WIDTH, XATTN_WIDTH))
    k = k.reshape(bsz, -1, XATTN_HEADS, XATTN_HEAD_DIM)
    v = v.reshape(bsz, -1, XATTN_HEADS, XATTN_HEAD_DIM)
    s = jnp.einsum('bqhd,bkhd->bhqk', q, k).astype(jnp.float32) * (XATTN_HEAD_DIM ** -0.5)
    p = jax.nn.softmax(s, axis=-1).astype(v.dtype)
    o = jnp.einsum('bhqk,bkhd->bqhd', p, v).reshape(bsz, seq, XATTN_WIDTH)
    return o @ w_o


def squared_relu_mlp(hn, w_up, w_down):
    return jnp.square(jax.nn.relu(hn @ w_up)) @ w_down


def setup_inputs(seed: int = 0) -> dict:
    key = jax.random.key(seed)
    ks = iter(jax.random.split(key, 32))
    f32 = jnp.float32

    def w(shape, fan_in):
        return jax.random.normal(next(ks), shape, f32) * fan_in ** -0.5

    def gain(shape):
        return 1.0 + 0.02 * jax.random.normal(next(ks), shape, f32)

    x = jax.random.normal(next(ks), (BATCH, SEQ, D_MODEL), f32)
    mem = jax.random.normal(next(ks), (BATCH, MEM_LEN, D_MODEL), f32)
    mem_norm = gain((D_MODEL,))
    norm_mix = gain((DEPTH, D_MODEL))
    norm_xattn = gain((DEPTH, D_MODEL))
    norm_mlp = gain((DEPTH, D_MODEL))
    norm_final = gain((D_MODEL,))
    ev_in_proj = w((N_EVEN, D_MODEL, EVEN_PROJ), D_MODEL)
    fox_fgate_bias = FOX_FGATE_BIAS_CENTER + 0.5 * jax.random.normal(next(ks), (N_EVEN, FOX_HEADS), f32)
    hgrn_lb_logits = 0.5 * jax.random.normal(next(ks), (N_EVEN, HGRN_KEY_WIDTH), f32)
    hgrn_out_norm = gain((N_EVEN, HGRN_VAL_WIDTH))
    ev_out_proj = w((N_EVEN, EVEN_MIX_WIDTH, D_MODEL), EVEN_MIX_WIDTH)
    ssm_in_proj = w((N_ODD, D_MODEL, SSM_PROJ), D_MODEL)
    ssm_conv_w = w((N_ODD, SSM_CONV, SSM_CONV_CH), SSM_CONV)
    ssm_conv_b = 0.02 * jax.random.normal(next(ks), (N_ODD, SSM_CONV_CH), f32)
    u = jax.random.uniform(next(ks), (N_ODD, SSM_HEADS), f32)
    dt0 = jnp.exp(u * (np.log(SSM_DT_MAX) - np.log(SSM_DT_MIN)) + np.log(SSM_DT_MIN))
    ssm_dt_bias = dt0 + jnp.log(-jnp.expm1(-dt0))
    ssm_A_log = jnp.log(jax.random.uniform(next(ks), (N_ODD, SSM_HEADS), f32, 1.0, 16.0))
    ssm_D = 1.0 + 0.1 * jax.random.normal(next(ks), (N_ODD, SSM_HEADS), f32)
    ssm_norm = gain((N_ODD, SSM_INNER))
    ssm_out_proj = w((N_ODD, SSM_INNER, D_MODEL), SSM_INNER)
    xa_q = w((DEPTH, D_MODEL, XATTN_WIDTH), D_MODEL)
    xa_kv = w((DEPTH, D_MODEL, 2 * XATTN_WIDTH), D_MODEL)
    xa_o = w((DEPTH, XATTN_WIDTH, D_MODEL), XATTN_WIDTH)
    mlp_up = w((DEPTH, D_MODEL, D_FF), D_MODEL)
    mlp_down = w((DEPTH, D_FF, D_MODEL), D_FF)
    return {'x': x, 'mem': mem, 'mem_norm': mem_norm, 'norm_mix': norm_mix,
            'norm_xattn': norm_xattn, 'norm_mlp': norm_mlp, 'norm_final': norm_final,
            'ev_in_proj': ev_in_proj, 'fox_fgate_bias': fox_fgate_bias,
            'hgrn_lb_logits': hgrn_lb_logits, 'hgrn_out_norm': hgrn_out_norm,
            'ev_out_proj': ev_out_proj, 'ssm_in_proj': ssm_in_proj, 'ssm_conv_w': ssm_conv_w,
            'ssm_conv_b': ssm_conv_b, 'ssm_dt_bias': ssm_dt_bias, 'ssm_A_log': ssm_A_log,
            'ssm_D': ssm_D, 'ssm_norm': ssm_norm, 'ssm_out_proj': ssm_out_proj,
            'xa_q': xa_q, 'xa_kv': xa_kv, 'xa_o': xa_o, 'mlp_up': mlp_up, 'mlp_down': mlp_down}


def reference(x, mem, mem_norm, norm_mix, norm_xattn, norm_mlp, norm_final,
              ev_in_proj, fox_fgate_bias, hgrn_lb_logits, hgrn_out_norm, ev_out_proj,
              ssm_in_proj, ssm_conv_w, ssm_conv_b, ssm_dt_bias, ssm_A_log, ssm_D, ssm_norm,
              ssm_out_proj, xa_q, xa_kv, xa_o, mlp_up, mlp_down):
    mem_n = rmsnorm(mem, mem_norm)
    lb_w = jax.nn.softmax(hgrn_lb_logits.astype(jnp.float32), axis=0)
    lb_all = jnp.cumsum(lb_w, axis=0) - lb_w[0:1]
    h = x
    for layer in range(DEPTH):
        hn = rmsnorm(h, norm_mix[layer])
        if layer % 2 == 0:
            e = layer // 2
            mix = fox_hgrn2_mixer(hn, ev_in_proj[e], fox_fgate_bias[e], lb_all[e],
                                  hgrn_out_norm[e], ev_out_proj[e])
        else:
            o = layer // 2
            mix = mamba2_mixer(hn, ssm_in_proj[o], ssm_conv_w[o], ssm_conv_b[o], ssm_dt_bias[o],
                               ssm_A_log[o], ssm_D[o], ssm_norm[o], ssm_out_proj[o])
        h = h + mix
        h = h + memory_cross_attention(rmsnorm(h, norm_xattn[layer]), mem_n,
                                       xa_q[layer], xa_kv[layer], xa_o[layer])
        h = h + squared_relu_mlp(rmsnorm(h, norm_mlp[layer]), mlp_up[layer], mlp_down[layer])
    return rmsnorm(h, norm_final)
```

```python
import functools

import jax
import jax.numpy as jnp
from jax import lax
from jax.experimental import pallas as pl
from jax.experimental.pallas import tpu as pltpu

F32 = jnp.float32
BF16 = jnp.bfloat16
HIGHEST = lax.Precision.HIGHEST

NORM_EPS = 1e-6
MASK_VALUE = -1e30
LOG_FLOOR = 1e-30

LANES = 128
VMEM_LIMIT_BYTES = 56 * 1024 * 1024

FOX_HEADS = 8
FOX_HEAD_DIM = 64
FOX_WIDTH = FOX_HEADS * FOX_HEAD_DIM
FOX_PAIRS = FOX_WIDTH // LANES
FOX_TILE = 256

HGRN_HEADS = 4
HGRN_DIM = 128
HGRN_WIDTH = HGRN_HEADS * HGRN_DIM
HGRN_CHUNK = 64
HGRN_SUB = 16
HGRN_ROWS = 256

SSM_INNER = 2048
SSM_HEAD_DIM = 64
SSM_HEADS = SSM_INNER // SSM_HEAD_DIM
SSM_GROUPS = 8
SSM_GROUP_WIDTH = SSM_INNER // SSM_GROUPS
SSM_STATE = 128
SSM_CONV = 4
SSM_CHUNK = 128
SSM_BC = SSM_GROUPS * SSM_STATE

XATTN_HEADS = 4
XATTN_HEAD_DIM = 128
XATTN_WIDTH = XATTN_HEADS * XATTN_HEAD_DIM

ROW_TILE = 512
MLP_ROW_TILE = 1024
MLP_FF_TILE = 512


def _params(*semantics):
    return pltpu.CompilerParams(dimension_semantics=semantics, vmem_limit_bytes=VMEM_LIMIT_BYTES)


def _rms(x, gain):
    ms = jnp.mean(x * x, axis=-1, keepdims=True)
    return x * lax.rsqrt(ms + NORM_EPS) * gain


def _log_sigmoid(x):
    return jnp.minimum(x, 0.0) - jnp.log1p(jnp.exp(-jnp.abs(x)))


def _sigmoid(x):
    return 1.0 / (1.0 + jnp.exp(-x))


def _softplus(x):
    return jnp.maximum(x, 0.0) + jnp.log1p(jnp.exp(-jnp.abs(x)))


def _dot(a, b):
    return jnp.dot(a, b, preferred_element_type=F32)


def _dot_nt(a, b):
    return lax.dot_general(a, b, (((1,), (1,)), ((), ())), preferred_element_type=F32)


def _cumsum_rows(x):
    n = x.shape[0]
    r = lax.broadcasted_iota(jnp.int32, (n, n), 0)
    c = lax.broadcasted_iota(jnp.int32, (n, n), 1)
    tri = jnp.where(r >= c, 1.0, 0.0).astype(F32)
    return jnp.dot(tri, x, precision=HIGHEST, preferred_element_type=F32)


def _norm_matmul_kernel(h_ref, g_ref, w_ref, o_ref, hn_ref):
    @pl.when(pl.program_id(1) == 0)
    def _():
        hn_ref[...] = _rms(h_ref[...], g_ref[...]).astype(BF16)

    o_ref[...] = _dot(hn_ref[...], w_ref[...]).astype(o_ref.dtype)


def norm_matmul(h, gain, w, out_dtype, tn):
    rows, d = h.shape
    n = w.shape[1]
    tm = min(ROW_TILE * 2, rows)
    return pl.pallas_call(
        _norm_matmul_kernel,
        grid=(rows // tm, n // tn),
        in_specs=[pl.BlockSpec((tm, d), lambda i, j: (i, 0)),
                  pl.BlockSpec((1, d), lambda i, j: (0, 0)),
                  pl.BlockSpec((d, tn), lambda i, j: (0, j))],
        out_specs=pl.BlockSpec((tm, tn), lambda i, j: (i, j)),
        out_shape=jax.ShapeDtypeStruct((rows, n), out_dtype),
        scratch_shapes=[pltpu.VMEM((tm, d), BF16)],
        compiler_params=_params("parallel", "arbitrary"),
        name="norm_matmul",
    )(h, gain.reshape(1, d), w)


def _matmul_residual_kernel(*refs, n_in):
    a_refs, w_refs = refs[:n_in], refs[n_in:2 * n_in]
    h_ref, o_ref = refs[2 * n_in], refs[2 * n_in + 1]
    acc = h_ref[...]
    for a_ref, w_ref in zip(a_refs, w_refs):
        acc = acc + _dot(a_ref[...], w_ref[...])
    o_ref[...] = acc


def matmul_residual(acts, weights, h):
    rows, d = h.shape
    tm = min(ROW_TILE, rows)
    n_in = len(acts)
    in_specs = [pl.BlockSpec((tm, a.shape[1]), lambda i: (i, 0)) for a in acts]
    in_specs += [pl.BlockSpec(w.shape, lambda i: (0, 0)) for w in weights]
    in_specs += [pl.BlockSpec((tm, d), lambda i: (i, 0))]
    return pl.pallas_call(
        functools.partial(_matmul_residual_kernel, n_in=n_in),
        grid=(rows // tm,),
        in_specs=in_specs,
        out_specs=pl.BlockSpec((tm, d), lambda i: (i, 0)),
        out_shape=jax.ShapeDtypeStruct((rows, d), F32),
        compiler_params=_params("parallel"),
        name="matmul_residual",
    )(*acts, *weights, h)


def _xattn_kernel(h_ref, g_ref, wq_ref, kv_ref, wo_ref, o_ref):
    x = h_ref[...]
    hn = _rms(x, g_ref[...]).astype(BF16)
    q = _dot(hn, wq_ref[...]).astype(BF16)
    scale = XATTN_HEAD_DIM ** -0.5
    outs = []
    for hd in range(XATTN_HEADS):
        lo = hd * XATTN_HEAD_DIM
        qh = q[:, lo:lo + XATTN_HEAD_DIM]
        kh = kv_ref[:, lo:lo + XATTN_HEAD_DIM]
        vh = kv_ref[:, XATTN_WIDTH + lo:XATTN_WIDTH + lo + XATTN_HEAD_DIM]
        s = _dot_nt(qh, kh) * scale
        p = jnp.exp(s - jnp.max(s, axis=-1, keepdims=True))
        denom = jnp.sum(p, axis=-1, keepdims=True)
        outs.append((_dot(p.astype(BF16), vh) / denom).astype(BF16))
    o_ref[...] = x + _dot(jnp.concatenate(outs, axis=-1), wo_ref[...])


def cross_attention(h, gain, wq, kv, wo, batch):
    rows, d = h.shape
    seq = rows // batch
    mem_len = kv.shape[0] // batch
    tm = min(ROW_TILE, seq)
    nt = seq // tm
    return pl.pallas_call(
        _xattn_kernel,
        grid=(batch, nt),
        in_specs=[pl.BlockSpec((tm, d), lambda b, i: (b * nt + i, 0)),
                  pl.BlockSpec((1, d), lambda b, i: (0, 0)),
                  pl.BlockSpec(wq.shape, lambda b, i: (0, 0)),
                  pl.BlockSpec((mem_len, 2 * XATTN_WIDTH), lambda b, i: (b, 0)),
                  pl.BlockSpec(wo.shape, lambda b, i: (0, 0))],
        out_specs=pl.BlockSpec((tm, d), lambda b, i: (b * nt + i, 0)),
        out_shape=jax.ShapeDtypeStruct((rows, d), F32),
        compiler_params=_params("parallel", "parallel"),
        name="cross_attention",
    )(h, gain.reshape(1, d), wq, kv, wo)


def _mlp_kernel(h_ref, g_ref, wu_ref, wd_ref, gf_ref, o_ref, hn_ref, *, final_norm):
    f = pl.program_id(1)

    @pl.when(f == 0)
    def _():
        x = h_ref[...]
        hn_ref[...] = _rms(x, g_ref[...]).astype(BF16)
        o_ref[...] = x

    u = _dot(hn_ref[...], wu_ref[...])
    u = jnp.square(jnp.maximum(u, 0.0)).astype(BF16)
    o_ref[...] += _dot(u, wd_ref[...])

    if final_norm:
        @pl.when(f == pl.num_programs(1) - 1)
        def _():
            o_ref[...] = _rms(o_ref[...], gf_ref[...])


def mlp(h, gain, w_up, w_down, final_gain, final_norm):
    rows, d = h.shape
    ff = w_up.shape[1]
    tm = min(MLP_ROW_TILE, rows)
    tf = min(MLP_FF_TILE, ff)
    return pl.pallas_call(
        functools.partial(_mlp_kernel, final_norm=final_norm),
        grid=(rows // tm, ff // tf),
        in_specs=[pl.BlockSpec((tm, d), lambda i, f: (i, 0)),
                  pl.BlockSpec((1, d), lambda i, f: (0, 0)),
                  pl.BlockSpec((d, tf), lambda i, f: (0, f)),
                  pl.BlockSpec((tf, d), lambda i, f: (f, 0)),
                  pl.BlockSpec((1, d), lambda i, f: (0, 0))],
        out_specs=pl.BlockSpec((tm, d), lambda i, f: (i, 0)),
        out_shape=jax.ShapeDtypeStruct((rows, d), F32),
        scratch_shapes=[pltpu.VMEM((tm, d), BF16)],
        compiler_params=_params("parallel", "arbitrary"),
        name="mlp",
    )(h, gain.reshape(1, d), w_up, w_down, final_gain.reshape(1, d))


def _fox_gate_kernel(fg_ref, bias_ref, ccol_ref, crow_ref):
    seq = fg_ref.shape[0]
    blk = FOX_TILE
    carry = jnp.zeros((1, LANES), F32)
    for i in range(seq // blk):
        rows = slice(i * blk, (i + 1) * blk)
        c = _cumsum_rows(_log_sigmoid(fg_ref[rows, :] + bias_ref[...])) + carry
        ccol_ref[rows, :] = c
        crow_ref[0, :, rows] = c.T[:FOX_HEADS, :]
        carry = c[blk - 1:blk, :]


def fox_gates(gates, bias, batch, col_block):
    rows = gates.shape[0]
    seq = rows // batch
    return pl.pallas_call(
        _fox_gate_kernel,
        grid=(batch,),
        in_specs=[pl.BlockSpec((seq, LANES), lambda b: (b, col_block)),
                  pl.BlockSpec((1, LANES), lambda b: (0, 0))],
        out_specs=[pl.BlockSpec((seq, LANES), lambda b: (b, 0)),
                   pl.BlockSpec((1, FOX_HEADS, seq), lambda b: (b, 0, 0))],
        out_shape=[jax.ShapeDtypeStruct((rows, LANES), F32),
                   jax.ShapeDtypeStruct((batch, FOX_HEADS, seq), F32)],
        compiler_params=_params("parallel"),
        name="fox_gates",
    )(gates, bias)


def _fox_kernel(q_ref, k_ref, v_ref, ccol_ref, crow_ref, o_ref, m_ref, l_ref, acc_ref):
    pair = pl.program_id(1)
    qi = pl.program_id(2)
    tq, tk = q_ref.shape[0], FOX_TILE
    lane = lax.broadcasted_iota(jnp.int32, (tq, LANES), 1)
    first = lane < FOX_HEAD_DIM
    q = q_ref[...]
    zero = jnp.zeros_like(q)
    qs = (jnp.where(first, q, zero), jnp.where(first, zero, q))
    ccol = ccol_ref[...]
    cts = [jnp.sum(jnp.where(lane == 2 * pair + hh, ccol, 0.0), axis=-1, keepdims=True)
           for hh in range(2)]

    m_ref[...] = jnp.full(m_ref.shape, MASK_VALUE, F32)
    l_ref[...] = jnp.zeros(l_ref.shape, F32)
    acc_ref[...] = jnp.zeros(acc_ref.shape, F32)

    def step(j, masked):
        start = pl.multiple_of(j * tk, tk)
        k = k_ref[pl.ds(start, tk), :]
        v = v_ref[pl.ds(start, tk), :]
        for hh in range(2):
            crow = crow_ref[0, pl.ds(2 * pair + hh, 1), pl.ds(start, tk)]
            s = _dot_nt(qs[hh], k) + (cts[hh] - crow)
            if masked:
                r = lax.broadcasted_iota(jnp.int32, (tq, tk), 0)
                c = lax.broadcasted_iota(jnp.int32, (tq, tk), 1)
                s = jnp.where(r >= c, s, MASK_VALUE)
            m_old = m_ref[hh]
            m_new = jnp.maximum(m_old, jnp.max(s, axis=-1, keepdims=True))
            alpha = jnp.exp(m_old - m_new)
            p = jnp.exp(s - m_new)
            l_ref[hh] = alpha * l_ref[hh] + jnp.sum(p, axis=-1, keepdims=True)
            acc_ref[hh] = alpha * acc_ref[hh] + _dot(p.astype(BF16), v)
            m_ref[hh] = m_new

    def body(j, carry):
        step(j, masked=False)
        return carry

    lax.fori_loop(0, qi, body, 0)
    step(qi, masked=True)
    out = jnp.where(first, acc_ref[0] / l_ref[0], acc_ref[1] / l_ref[1])
    o_ref[...] = out.astype(o_ref.dtype)


def fox_attention(proj, ccol, crow, batch):
    rows = proj.shape[0]
    seq = rows // batch
    tq = FOX_TILE
    nq = seq // tq
    return pl.pallas_call(
        _fox_kernel,
        grid=(batch, FOX_PAIRS, nq),
        in_specs=[pl.BlockSpec((tq, LANES), lambda b, p, i: (b * nq + i, p)),
                  pl.BlockSpec((seq, LANES), lambda b, p, i: (b, FOX_PAIRS + p)),
                  pl.BlockSpec((seq, LANES), lambda b, p, i: (b, 2 * FOX_PAIRS + p)),
                  pl.BlockSpec((tq, LANES), lambda b, p, i: (b * nq + i, 0)),
                  pl.BlockSpec((1, FOX_HEADS, seq), lambda b, p, i: (b, 0, 0))],
        out_specs=pl.BlockSpec((tq, LANES), lambda b, p, i: (b * nq + i, p)),
        out_shape=jax.ShapeDtypeStruct((rows, FOX_WIDTH), BF16),
        scratch_shapes=[pltpu.VMEM((2, tq, 1), F32), pltpu.VMEM((2, tq, 1), F32),
                        pltpu.VMEM((2, tq, LANES), F32)],
        compiler_params=_params("parallel", "parallel", "arbitrary"),
        name="fox_attention",
    )(proj, proj, proj, ccol, crow)


def _hgrn_kernel(hq_ref, hi_ref, hg_ref, hf_ref, lbl_ref, gain_ref, o_ref, state_ref, *, layer):
    @pl.when(pl.program_id(2) == 0)
    def _():
        state_ref[...] = jnp.zeros(state_ref.shape, F32)

    logits = lbl_ref[...]
    w = jnp.exp(logits - jnp.max(logits, axis=0, keepdims=True))
    w = w / jnp.sum(w, axis=0, keepdims=True)
    lb = jnp.sum(w[:layer + 1], axis=0, keepdims=True) - w[0:1]
    log_lb = jnp.log(jnp.maximum(lb, LOG_FLOOR))
    log1m_lb = jnp.log1p(-lb)

    ch, sub = HGRN_CHUNK, HGRN_SUB
    n_sub = ch // sub
    lane_c = lax.broadcasted_iota(jnp.int32, (sub, ch), 1)
    row_c = lax.broadcasted_iota(jnp.int32, (sub, ch), 0)

    def chunk(ci, carry):
        r0 = pl.multiple_of(ci * ch, ch)
        rows = pl.ds(r0, ch)
        hf = hf_ref[rows, :]
        hq = hq_ref[rows, :].astype(F32)
        v = hi_ref[rows, :].astype(F32)
        hg = hg_ref[rows, :].astype(F32)

        log_gate = log1m_lb + _log_sigmoid(hf)
        log_f = jnp.maximum(log_lb, log_gate) + jnp.log1p(jnp.exp(-jnp.abs(log_lb - log_gate)))
        k = (1.0 - lb) * _sigmoid(-hf)
        q = hq * _sigmoid(hq)
        b = _cumsum_rows(log_f)

        bref = jnp.concatenate(
            [jnp.broadcast_to(b[max(i * sub - 1, 0):max(i * sub - 1, 0) + 1, :], (sub, HGRN_DIM))
             for i in range(n_sub)], axis=0)
        q_rel = (q * jnp.exp(b - bref)).astype(BF16)

        a_rows = []
        for i in range(n_sub):
            s0 = i * sub
            bq = b[s0:s0 + sub, :]
            qq = q[s0:s0 + sub, :]
            diag = jnp.zeros((sub, ch), F32)
            for s in range(sub):
                bs = b[s0 + s:s0 + s + 1, :]
                ks = k[s0 + s:s0 + s + 1, :]
                e = jnp.exp(jnp.minimum(bq - bs, 0.0))
                col = jnp.sum(qq * ks * e, axis=-1, keepdims=True)
                diag = jnp.where(lane_c == s0 + s, col, diag)
            diag = jnp.where(row_c + s0 >= lane_c, diag, 0.0)
            if i == 0:
                a_rows.append(diag)
            else:
                k_rel = (k * jnp.exp(jnp.minimum(bref[s0:s0 + 1, :] - b, 0.0))).astype(BF16)
                off = _dot_nt(q_rel[s0:s0 + sub, :], k_rel)
                a_rows.append(jnp.where(lane_c < s0, off, diag))
        a = jnp.concatenate(a_rows, axis=0).astype(BF16)

        state_t = state_ref[...]
        b_last = b[ch - 1:ch, :]
        out = _dot(a, v.astype(BF16)) + _dot_nt((q * jnp.exp(b)).astype(BF16), state_t.astype(BF16))
        k_end = (k * jnp.exp(b_last - b)).astype(BF16)
        state_ref[...] = state_t * jnp.exp(b_last) + _dot(v.T.astype(BF16), k_end)

        y = out * lax.rsqrt(jnp.mean(out * out, axis=-1, keepdims=True) + NORM_EPS) * gain_ref[...]
        o_ref[rows, :] = (y * (hg * _sigmoid(hg))).astype(o_ref.dtype)
        return carry

    lax.fori_loop(0, o_ref.shape[0] // ch, chunk, 0)


def hgrn2(proj, gates, lb_logits, gain, batch, layer, col_q, col_i, col_g):
    rows = proj.shape[0]
    seq = rows // batch
    tr = min(HGRN_ROWS, seq)
    nt = seq // tr
    n_layers = lb_logits.shape[0]

    def tok(col0):
        return pl.BlockSpec((tr, LANES), lambda b, h, t: (b * nt + t, col0 + h))

    return pl.pallas_call(
        functools.partial(_hgrn_kernel, layer=layer),
        grid=(batch, HGRN_HEADS, nt),
        in_specs=[tok(col_q), tok(col_i), tok(col_g), tok(0),
                  pl.BlockSpec((n_layers, LANES), lambda b, h, t: (0, h)),
                  pl.BlockSpec((1, LANES), lambda b, h, t: (0, h))],
        out_specs=tok(0),
        out_shape=jax.ShapeDtypeStruct((rows, HGRN_WIDTH), BF16),
        scratch_shapes=[pltpu.VMEM((HGRN_DIM, HGRN_DIM), F32)],
        compiler_params=_params("parallel", "parallel", "arbitrary"),
        name="hgrn2",
    )(proj, proj, proj, gates, lb_logits, gain.reshape(1, HGRN_WIDTH))


def _causal_conv_silu(x, tail_ref, w, bias):
    tail = tail_ref[...]
    row8 = lax.broadcasted_iota(jnp.int32, tail.shape, 0)
    acc = x * w[SSM_CONV - 1:SSM_CONV, :] + bias
    for j in range(1, SSM_CONV):
        rolled = pltpu.roll(x, j, axis=0)
        head = jnp.where(row8 < j, pltpu.roll(tail, j, axis=0), rolled[0:8, :])
        shifted = jnp.concatenate([head, rolled[8:, :]], axis=0)
        acc = acc + shifted * w[SSM_CONV - 1 - j:SSM_CONV - j, :]
    tail_ref[...] = x[x.shape[0] - 8:, :]
    return acc * _sigmoid(acc)


def _pair_select(first, a, b):
    return jnp.where(first, a, b)


def _ssd_kernel(z_ref, x_ref, bc_ref, dt_ref, cwx_ref, cwbc_ref, cbx_ref, cbbc_ref, dtb_ref, alog_ref,
                dskip_ref, gain_ref, o_ref, state_ref, tailx_ref, tailbc_ref):
    @pl.when(pl.program_id(1) == 0)
    def _():
        state_ref[...] = jnp.zeros(state_ref.shape, F32)
        tailx_ref[...] = jnp.zeros(tailx_ref.shape, F32)
        tailbc_ref[...] = jnp.zeros(tailbc_ref.shape, F32)

    ch = SSM_CHUNK
    dt = _softplus(dt_ref[...] + dtb_ref[...])
    a = -jnp.exp(alog_ref[...])
    cum = _cumsum_rows(dt * a)
    cum_t = cum.T
    dt_t = dt.T
    exp_cum = jnp.exp(cum)
    w_end = jnp.exp(cum[ch - 1:ch, :] - cum) * dt

    r = lax.broadcasted_iota(jnp.int32, (ch, ch), 0)
    c = lax.broadcasted_iota(jnp.int32, (ch, ch), 1)
    causal = r >= c
    first = lax.broadcasted_iota(jnp.int32, (ch, LANES), 1) < SSM_HEAD_DIM
    first_row = first[0:1, :]

    def bcast(col):
        return jnp.broadcast_to(col, (ch, LANES))

    for g in range(SSM_GROUPS):
        gx = slice(g * SSM_GROUP_WIDTH, (g + 1) * SSM_GROUP_WIDTH)
        gb = slice(g * SSM_STATE, (g + 1) * SSM_STATE)
        gc = slice(SSM_BC + g * SSM_STATE, SSM_BC + (g + 1) * SSM_STATE)
        xs = _causal_conv_silu(x_ref[:, gx].astype(F32), tailx_ref.at[:, gx], cwx_ref[:, gx], cbx_ref[:, gx])
        bm = _causal_conv_silu(bc_ref[:, gb].astype(F32), tailbc_ref.at[:, gb], cwbc_ref[:, gb], cbbc_ref[:, gb])
        cm = _causal_conv_silu(bc_ref[:, gc].astype(F32), tailbc_ref.at[:, gc], cwbc_ref[:, gc], cbbc_ref[:, gc])
        bm16, cm16 = bm.astype(BF16), cm.astype(BF16)
        cb = _dot_nt(cm16, bm16)
        state = state_ref[g]
        y_state = _dot(cm16, state.astype(BF16))

        y_pairs, wx_pairs, decay_pairs = [], [], []
        for pr in range(2):
            xp = xs[:, pr * LANES:(pr + 1) * LANES]
            xp16 = xp.astype(BF16)
            ys, scale_in, scale_w, ends = [], [], [], []
            for hh in range(2):
                h = g * 4 + pr * 2 + hh
                ccol = bcast(cum[:, h:h + 1])
                seg = ccol - cum_t[h:h + 1, :]
                decay = jnp.where(causal, jnp.exp(jnp.minimum(seg, 0.0)), 0.0)
                m = (cb * decay * dt_t[h:h + 1, :]).astype(BF16)
                ys.append(_dot(m, xp16))
                scale_in.append(bcast(exp_cum[:, h:h + 1]))
                scale_w.append(bcast(w_end[:, h:h + 1]))
                ends.append(ccol[ch - 1:ch, :])
            y_in = y_state[:, pr * LANES:(pr + 1) * LANES] * _pair_select(first, scale_in[0], scale_in[1])
            y_pairs.append(_pair_select(first, ys[0], ys[1]) + y_in)
            wx_pairs.append((xp * _pair_select(first, scale_w[0], scale_w[1])).astype(BF16))
            decay_pairs.append(jnp.exp(_pair_select(first_row, ends[0], ends[1])))
        y = jnp.concatenate(y_pairs, axis=-1)
        wx = jnp.concatenate(wx_pairs, axis=-1)
        state_ref[g] = state * jnp.concatenate(decay_pairs, axis=-1) + _dot(bm.T.astype(BF16), wx)

        y = y + dskip_ref[:, gx] * xs
        zg = z_ref[:, gx].astype(F32)
        y = y * (zg * _sigmoid(zg))
        y = y * lax.rsqrt(jnp.mean(y * y, axis=-1, keepdims=True) + NORM_EPS) * gain_ref[:, gx]
        o_ref[:, gx] = y.astype(o_ref.dtype)


def ssd_mixer(proj, dt_raw, conv_w, conv_b, dt_bias, a_log, d_skip, gain, batch):
    rows = proj.shape[0]
    seq = rows // batch
    ch = SSM_CHUNK
    nc = seq // ch

    def tok(width, col):
        return pl.BlockSpec((ch, width), lambda b, c: (b * nc + c, col))

    def par(nrows, width, col):
        return pl.BlockSpec((nrows, width), lambda b, c: (0, col))

    return pl.pallas_call(
        _ssd_kernel,
        grid=(batch, nc),
        in_specs=[tok(SSM_INNER, 0), tok(SSM_INNER, 1), tok(2 * SSM_BC, 2), tok(LANES, 0),
                  par(SSM_CONV, SSM_INNER, 0), par(SSM_CONV, 2 * SSM_BC, 1),
                  par(1, SSM_INNER, 0), par(1, 2 * SSM_BC, 1),
                  par(1, LANES, 0), par(1, LANES, 0), par(1, SSM_INNER, 0), par(1, SSM_INNER, 0)],
        out_specs=tok(SSM_INNER, 0),
        out_shape=jax.ShapeDtypeStruct((rows, SSM_INNER), BF16),
        scratch_shapes=[pltpu.VMEM((SSM_GROUPS, SSM_STATE, SSM_GROUP_WIDTH), F32),
                        pltpu.VMEM((8, SSM_INNER), F32), pltpu.VMEM((8, 2 * SSM_BC), F32)],
        compiler_params=_params("parallel", "arbitrary"),
        name="ssd_mixer",
    )(proj, proj, proj, dt_raw, conv_w, conv_w, conv_b, conv_b, dt_bias, a_log, d_skip, gain)


def _pad_lanes(x):
    return jnp.pad(x, ((0, 0), (0, LANES - x.shape[1])))


def _split_cols(w, sizes):
    out, lo = [], 0
    for s in sizes:
        out.append(w[:, lo:lo + s])
        lo += s
    return out


def kernel(x, mem, mem_norm, norm_mix, norm_xattn, norm_mlp, norm_final, ev_in_proj, fox_fgate_bias,
           hgrn_lb_logits, hgrn_out_norm, ev_out_proj, ssm_in_proj, ssm_conv_w, ssm_conv_b, ssm_dt_bias,
           ssm_A_log, ssm_D, ssm_norm, ssm_out_proj, xa_q, xa_kv, xa_o, mlp_up, mlp_down):
    batch, seq, d = x.shape
    depth = norm_mix.shape[0]
    h = x.reshape(batch * seq, d)
    mem2 = mem.reshape(-1, d)

    for layer in range(depth):
        if layer % 2 == 0:
            e = layer // 2
            fq, fk, fv, fg, hq, hf, hi, hg = _split_cols(
                ev_in_proj[e], (FOX_WIDTH, FOX_WIDTH, FOX_WIDTH, FOX_HEADS,
                                HGRN_WIDTH, HGRN_WIDTH, HGRN_WIDTH, HGRN_WIDTH))
            w_main = jnp.concatenate([fq * FOX_HEAD_DIM ** -0.5, fk, fv, hq, hi, hg], axis=1).astype(BF16)
            w_gate = jnp.concatenate([hf, _pad_lanes(fg)], axis=1).astype(BF16)
            proj = norm_matmul(h, norm_mix[layer], w_main, BF16, 512)
            gates = norm_matmul(h, norm_mix[layer], w_gate, F32, w_gate.shape[1])
            ccol, crow = fox_gates(gates, _pad_lanes(fox_fgate_bias[e].reshape(1, -1)), batch,
                                   HGRN_WIDTH // LANES)
            a_out = fox_attention(proj, ccol, crow, batch)
            blocks = FOX_WIDTH // LANES
            b_out = hgrn2(proj, gates, hgrn_lb_logits, hgrn_out_norm[e], batch, e,
                          3 * blocks, 3 * blocks + HGRN_HEADS, 3 * blocks + 2 * HGRN_HEADS)
            w_out = ev_out_proj[e].astype(BF16)
            h = matmul_residual([a_out, b_out], [w_out[:FOX_WIDTH], w_out[FOX_WIDTH:]], h)
        else:
            o = layer // 2
            w_in = ssm_in_proj[o]
            n_main = 2 * SSM_INNER + 2 * SSM_BC
            proj = norm_matmul(h, norm_mix[layer], w_in[:, :n_main].astype(BF16), BF16, 512)
            dt_raw = norm_matmul(h, norm_mix[layer], _pad_lanes(w_in[:, n_main:]).astype(BF16), F32, LANES)
            y = ssd_mixer(proj, dt_raw, ssm_conv_w[o], ssm_conv_b[o].reshape(1, -1),
                          _pad_lanes(ssm_dt_bias[o].reshape(1, -1)), _pad_lanes(ssm_A_log[o].reshape(1, -1)),
                          jnp.repeat(ssm_D[o], SSM_HEAD_DIM).reshape(1, -1), ssm_norm[o].reshape(1, -1), batch)
            h = matmul_residual([y], [ssm_out_proj[o].astype(BF16)], h)

        kv = norm_matmul(mem2, mem_norm, xa_kv[layer].astype(BF16), BF16, 512)
        h = cross_attention(h, norm_xattn[layer], xa_q[layer].astype(BF16), kv, xa_o[layer].astype(BF16), batch)
        last = layer == depth - 1
        h = mlp(h, norm_mlp[layer], mlp_up[layer].astype(BF16), mlp_down[layer].astype(BF16), norm_final, last)

    return h.reshape(batch, seq, d)
```

```python
import functools

import jax
import jax.numpy as jnp
from jax import lax
from jax.experimental import pallas as pl
from jax.experimental.pallas import tpu as pltpu

F32 = jnp.float32
BF16 = jnp.bfloat16
HIGHEST = lax.Precision.HIGHEST

NORM_EPS = 1e-6
MASK_VALUE = -1e30
LOG_FLOOR = 1e-30
LOG2E = 1.4426950408889634

LANES = 128
VMEM_LIMIT_BYTES = 56 * 1024 * 1024

FOX_HEADS = 8
FOX_HEAD_DIM = 64
FOX_WIDTH = FOX_HEADS * FOX_HEAD_DIM
FOX_PAIRS = FOX_WIDTH // LANES
FOX_TILE = 256
FOX_TQ = 512
FOX_TK = 256

HGRN_HEADS = 4
HGRN_DIM = 128
HGRN_WIDTH = HGRN_HEADS * HGRN_DIM
HGRN_CHUNK = 64
HGRN_SUB = 16
HGRN_ROWS = 256

SSM_INNER = 2048
SSM_HEAD_DIM = 64
SSM_HEADS = SSM_INNER // SSM_HEAD_DIM
SSM_GROUPS = 8
SSM_GROUP_WIDTH = SSM_INNER // SSM_GROUPS
SSM_STATE = 128
SSM_CONV = 4
SSM_CHUNK = 128
SSM_BC = SSM_GROUPS * SSM_STATE

XATTN_HEADS = 4
XATTN_HEAD_DIM = 128
XATTN_WIDTH = XATTN_HEADS * XATTN_HEAD_DIM

ROW_TILE = 512
MLP_ROW_TILE = 1024
MLP_FF_TILE = 512


def _params(*semantics):
    return pltpu.CompilerParams(dimension_semantics=semantics, vmem_limit_bytes=VMEM_LIMIT_BYTES)


def _rms(x, gain):
    ms = jnp.mean(x * x, axis=-1, keepdims=True)
    return x * lax.rsqrt(ms + NORM_EPS) * gain


def _log_sigmoid(x):
    return jnp.minimum(x, 0.0) - jnp.log1p(jnp.exp(-jnp.abs(x)))


def _sigmoid(x):
    return 1.0 / (1.0 + jnp.exp(-x))


def _softplus(x):
    return jnp.maximum(x, 0.0) + jnp.log1p(jnp.exp(-jnp.abs(x)))


def _dot(a, b):
    return jnp.dot(a, b, preferred_element_type=F32)


def _dot_nt(a, b):
    return lax.dot_general(a, b, (((1,), (1,)), ((), ())), preferred_element_type=F32)


def _cumsum_rows(x):
    n = x.shape[0]
    r = lax.broadcasted_iota(jnp.int32, (n, n), 0)
    c = lax.broadcasted_iota(jnp.int32, (n, n), 1)
    tri = jnp.where(r >= c, 1.0, 0.0).astype(F32)
    return jnp.dot(tri, x, precision=HIGHEST, preferred_element_type=F32)


def _norm_matmul_kernel(h_ref, g_ref, w_ref, o_ref, hn_ref):
    @pl.when(pl.program_id(1) == 0)
    def _():
        hn_ref[...] = _rms(h_ref[...], g_ref[...]).astype(BF16)

    o_ref[...] = _dot(hn_ref[...], w_ref[...]).astype(o_ref.dtype)


def norm_matmul(h, gain, w, out_dtype, tn):
    rows, d = h.shape
    n = w.shape[1]
    tm = min(ROW_TILE * 2, rows)
    return pl.pallas_call(
        _norm_matmul_kernel,
        grid=(rows // tm, n // tn),
        in_specs=[pl.BlockSpec((tm, d), lambda i, j: (i, 0)),
                  pl.BlockSpec((1, d), lambda i, j: (0, 0)),
                  pl.BlockSpec((d, tn), lambda i, j: (0, j))],
        out_specs=pl.BlockSpec((tm, tn), lambda i, j: (i, j)),
        out_shape=jax.ShapeDtypeStruct((rows, n), out_dtype),
        scratch_shapes=[pltpu.VMEM((tm, d), BF16)],
        compiler_params=_params("parallel", "arbitrary"),
        name="norm_matmul",
    )(h, gain.reshape(1, d), w)


def _matmul_residual_kernel(*refs, n_in):
    a_refs, w_refs = refs[:n_in], refs[n_in:2 * n_in]
    h_ref, o_ref = refs[2 * n_in], refs[2 * n_in + 1]
    acc = h_ref[...]
    for a_ref, w_ref in zip(a_refs, w_refs):
        acc = acc + _dot(a_ref[...], w_ref[...])
    o_ref[...] = acc


def matmul_residual(acts, weights, h):
    rows, d = h.shape
    tm = min(ROW_TILE, rows)
    n_in = len(acts)
    in_specs = [pl.BlockSpec((tm, a.shape[1]), lambda i: (i, 0)) for a in acts]
    in_specs += [pl.BlockSpec(w.shape, lambda i: (0, 0)) for w in weights]
    in_specs += [pl.BlockSpec((tm, d), lambda i: (i, 0))]
    return pl.pallas_call(
        functools.partial(_matmul_residual_kernel, n_in=n_in),
        grid=(rows // tm,),
        in_specs=in_specs,
        out_specs=pl.BlockSpec((tm, d), lambda i: (i, 0)),
        out_shape=jax.ShapeDtypeStruct((rows, d), F32),
        compiler_params=_params("parallel"),
        name="matmul_residual",
    )(*acts, *weights, h)


def _xattn_kernel(h_ref, g_ref, wq_ref, kv_ref, wo_ref, o_ref):
    x = h_ref[...]
    hn = _rms(x, g_ref[...]).astype(BF16)
    q = _dot(hn, wq_ref[...]).astype(BF16)
    scale = XATTN_HEAD_DIM ** -0.5
    outs = []
    for hd in range(XATTN_HEADS):
        lo = hd * XATTN_HEAD_DIM
        qh = q[:, lo:lo + XATTN_HEAD_DIM]
        kh = kv_ref[:, lo:lo + XATTN_HEAD_DIM]
        vh = kv_ref[:, XATTN_WIDTH + lo:XATTN_WIDTH + lo + XATTN_HEAD_DIM]
        s = _dot_nt(qh, kh) * scale
        p = jnp.exp(s - jnp.max(s, axis=-1, keepdims=True))
        denom = jnp.sum(p, axis=-1, keepdims=True)
        outs.append((_dot(p.astype(BF16), vh) / denom).astype(BF16))
    o_ref[...] = x + _dot(jnp.concatenate(outs, axis=-1), wo_ref[...])


def cross_attention(h, gain, wq, kv, wo, batch):
    rows, d = h.shape
    seq = rows // batch
    mem_len = kv.shape[0] // batch
    tm = min(ROW_TILE, seq)
    nt = seq // tm
    return pl.pallas_call(
        _xattn_kernel,
        grid=(batch, nt),
        in_specs=[pl.BlockSpec((tm, d), lambda b, i: (b * nt + i, 0)),
                  pl.BlockSpec((1, d), lambda b, i: (0, 0)),
                  pl.BlockSpec(wq.shape, lambda b, i: (0, 0)),
                  pl.BlockSpec((mem_len, 2 * XATTN_WIDTH), lambda b, i: (b, 0)),
                  pl.BlockSpec(wo.shape, lambda b, i: (0, 0))],
        out_specs=pl.BlockSpec((tm, d), lambda b, i: (b * nt + i, 0)),
        out_shape=jax.ShapeDtypeStruct((rows, d), F32),
        compiler_params=_params("parallel", "parallel"),
        name="cross_attention",
    )(h, gain.reshape(1, d), wq, kv, wo)


def _mlp_kernel(h_ref, g_ref, wu_ref, wd_ref, gf_ref, o_ref, hn_ref, *, final_norm):
    f = pl.program_id(1)

    @pl.when(f == 0)
    def _():
        x = h_ref[...]
        hn_ref[...] = _rms(x, g_ref[...]).astype(BF16)
        o_ref[...] = x

    u = _dot(hn_ref[...], wu_ref[...])
    u = jnp.square(jnp.maximum(u, 0.0)).astype(BF16)
    o_ref[...] += _dot(u, wd_ref[...])

    if final_norm:
        @pl.when(f == pl.num_programs(1) - 1)
        def _():
            o_ref[...] = _rms(o_ref[...], gf_ref[...])


def mlp(h, gain, w_up, w_down, final_gain, final_norm):
    rows, d = h.shape
    ff = w_up.shape[1]
    tm = min(MLP_ROW_TILE, rows)
    tf = min(MLP_FF_TILE, ff)
    return pl.pallas_call(
        functools.partial(_mlp_kernel, final_norm=final_norm),
        grid=(rows // tm, ff // tf),
        in_specs=[pl.BlockSpec((tm, d), lambda i, f: (i, 0)),
                  pl.BlockSpec((1, d), lambda i, f: (0, 0)),
                  pl.BlockSpec((d, tf), lambda i, f: (0, f)),
                  pl.BlockSpec((tf, d), lambda i, f: (f, 0)),
                  pl.BlockSpec((1, d), lambda i, f: (0, 0))],
        out_specs=pl.BlockSpec((tm, d), lambda i, f: (i, 0)),
        out_shape=jax.ShapeDtypeStruct((rows, d), F32),
        scratch_shapes=[pltpu.VMEM((tm, d), BF16)],
        compiler_params=_params("parallel", "arbitrary"),
        name="mlp",
    )(h, gain.reshape(1, d), w_up, w_down, final_gain.reshape(1, d))


def _fox_gate_kernel(fg_ref, bias_ref, ccol_ref, crow_ref):
    seq = fg_ref.shape[0]
    blk = FOX_TILE
    carry = jnp.zeros((1, LANES), F32)
    for i in range(seq // blk):
        rows = slice(i * blk, (i + 1) * blk)
        c = _cumsum_rows(_log_sigmoid(fg_ref[rows, :] + bias_ref[...])) + carry
        c2 = c * LOG2E
        ccol_ref[rows, :] = c2
        crow_ref[0, :, rows] = c2.T[:FOX_HEADS, :]
        carry = c[blk - 1:blk, :]


def fox_gates(gates, bias, batch, col_block):
    rows = gates.shape[0]
    seq = rows // batch
    return pl.pallas_call(
        _fox_gate_kernel,
        grid=(batch,),
        in_specs=[pl.BlockSpec((seq, LANES), lambda b: (b, col_block)),
                  pl.BlockSpec((1, LANES), lambda b: (0, 0))],
        out_specs=[pl.BlockSpec((seq, LANES), lambda b: (b, 0)),
                   pl.BlockSpec((1, FOX_HEADS, seq), lambda b: (b, 0, 0))],
        out_shape=[jax.ShapeDtypeStruct((rows, LANES), F32),
                   jax.ShapeDtypeStruct((batch, FOX_HEADS, seq), F32)],
        compiler_params=_params("parallel"),
        name="fox_gates",
    )(gates, bias)


FOX_ONES_ROWS = 16


def _fox_kernel(q_ref, k_ref, v_ref, ccol_ref, crow_ref, o_ref, vt_ref, cs_ref, st_ref, p_ref, alpha_ref,
                acc_ref, m_ref):
    pair = pl.program_id(1)
    qi = pl.program_id(2)
    tq, tk = FOX_TQ, FOX_TK
    seq = k_ref.shape[0]
    hd = FOX_HEAD_DIM

    @pl.when(qi == 0)
    def _():
        lane = lax.broadcasted_iota(jnp.int32, (tk, LANES), 1)
        for blk in range(seq // tk):
            rows = slice(blk * tk, (blk + 1) * tk)
            vt = v_ref[rows, :].astype(F32).T
            c = ccol_ref[rows, :]
            for hh in range(2):
                vt_ref[hh, 0:hd, rows] = vt[hh * hd:(hh + 1) * hd, :].astype(BF16)
                vt_ref[hh, hd:hd + FOX_ONES_ROWS, rows] = jnp.ones((FOX_ONES_ROWS, tk), BF16)
                col = jnp.sum(jnp.where(lane == 2 * pair + hh, c, 0.0), axis=-1, keepdims=True)
                cs_ref[hh, rows, :] = jnp.broadcast_to(col, (tk, LANES))

    n_kv = (qi + 1) * (tq // tk)
    q_start = qi * tq
    qt = q_ref[...].astype(F32).T
    row = lax.broadcasted_iota(jnp.int32, (LANES, tq), 0)
    qts = (jnp.where(row < hd, qt, 0.0).astype(BF16), jnp.where(row < hd, 0.0, qt).astype(BF16))
    cts = [crow_ref[0, pl.ds(2 * pair + hh, 1), pl.ds(pl.multiple_of(q_start, tq), tq)]
           for hh in range(2)]
    key_minus_query = (lax.broadcasted_iota(jnp.int32, (tk, tq), 0)
                       - lax.broadcasted_iota(jnp.int32, (tk, tq), 1))

    def scores(j):
        k = k_ref[pl.ds(pl.multiple_of(j * tk, tk), tk), :]
        return [_dot(k, qts[hh]) for hh in range(2)]

    def values_times_probs(j, par):
        start = pl.multiple_of(j * tk, tk)
        return [_dot(vt_ref[hh, :, pl.ds(start, tk)], p_ref[par, hh]) for hh in range(2)]

    m_ref[...] = jnp.full(m_ref.shape, MASK_VALUE, F32)
    acc_ref[...] = jnp.zeros(acc_ref.shape, F32)
    p_ref[1] = jnp.zeros(p_ref.shape[1:], BF16)
    alpha_ref[1] = jnp.ones(alpha_ref.shape[1:], F32)
    first_scores = scores(0)
    for hh in range(2):
        st_ref[0, hh] = first_scores[hh]

    def step(j, masked, has_next):
        par = lax.rem(j, 2)
        prev = 1 - par
        pv = values_times_probs(jnp.maximum(j - 1, 0), prev)
        if has_next:
            next_scores = scores(j + 1)
        start = pl.multiple_of(j * tk, tk)
        for hh in range(2):
            cs = cs_ref[hh, pl.ds(start, tk), :]
            st = st_ref[par, hh] + (cts[hh] - jnp.concatenate([cs] * (tq // LANES), axis=1))
            if masked:
                st = jnp.where(key_minus_query <= q_start - start, st, MASK_VALUE)
            m_old = m_ref[hh]
            m_new = jnp.maximum(m_old, jnp.max(st, axis=0, keepdims=True))
            p_ref[par, hh] = jnp.exp2(st - m_new).astype(BF16)
            acc_ref[hh] = alpha_ref[prev, hh] * acc_ref[hh] + pv[hh]
            alpha_ref[par, hh] = jnp.exp2(m_old - m_new)
            m_ref[hh] = m_new
        if has_next:
            for hh in range(2):
                st_ref[prev, hh] = next_scores[hh]

    def body(j, carry):
        step(j, masked=False, has_next=True)
        return carry

    n_diag = tq // tk
    lax.fori_loop(0, n_kv - n_diag, body, 0)
    for d in range(n_diag):
        step(n_kv - n_diag + d, masked=True, has_next=d < n_diag - 1)
    last = n_kv - 1
    last_par = lax.rem(last, 2)
    pv = values_times_probs(last, last_par)
    outs = []
    for hh in range(2):
        acc = alpha_ref[last_par, hh] * acc_ref[hh] + pv[hh]
        outs.append(acc[0:hd, :] / acc[hd:hd + 1, :])
    o_ref[...] = jnp.concatenate(outs, axis=0).T.astype(o_ref.dtype)


def fox_attention(proj, ccol, crow, batch):
    rows = proj.shape[0]
    seq = rows // batch
    tq, tk = FOX_TQ, FOX_TK
    nq = seq // tq
    acc_rows = FOX_HEAD_DIM + FOX_ONES_ROWS
    return pl.pallas_call(
        _fox_kernel,
        grid=(batch, FOX_PAIRS, nq),
        in_specs=[pl.BlockSpec((tq, LANES), lambda b, p, i: (b * nq + i, p)),
                  pl.BlockSpec((seq, LANES), lambda b, p, i: (b, FOX_PAIRS + p)),
                  pl.BlockSpec((seq, LANES), lambda b, p, i: (b, 2 * FOX_PAIRS + p)),
                  pl.BlockSpec((seq, LANES), lambda b, p, i: (b, 0)),
                  pl.BlockSpec((1, FOX_HEADS, seq), lambda b, p, i: (b, 0, 0))],
        out_specs=pl.BlockSpec((tq, LANES), lambda b, p, i: (b * nq + i, p)),
        out_shape=jax.ShapeDtypeStruct((rows, FOX_WIDTH), BF16),
        scratch_shapes=[pltpu.VMEM((2, acc_rows, seq), BF16), pltpu.VMEM((2, seq, LANES), F32),
                        pltpu.VMEM((2, 2, tk, tq), F32), pltpu.VMEM((2, 2, tk, tq), BF16),
                        pltpu.VMEM((2, 2, 1, tq), F32),
                        pltpu.VMEM((2, acc_rows, tq), F32), pltpu.VMEM((2, 1, tq), F32)],
        compiler_params=_params("parallel", "parallel", "arbitrary"),
        name="fox_attention",
    )(proj, proj, proj, ccol, crow)


def _hgrn_kernel(hq_ref, hi_ref, hg_ref, hf_ref, lbl_ref, gain_ref, o_ref, state_ref, *, layer):
    @pl.when(pl.program_id(2) == 0)
    def _():
        state_ref[...] = jnp.zeros(state_ref.shape, F32)

    logits = lbl_ref[...]
    w = jnp.exp(logits - jnp.max(logits, axis=0, keepdims=True))
    w = w / jnp.sum(w, axis=0, keepdims=True)
    lb = jnp.sum(w[:layer + 1], axis=0, keepdims=True) - w[0:1]
    log_lb = jnp.log(jnp.maximum(lb, LOG_FLOOR))
    log1m_lb = jnp.log1p(-lb)

    ch, sub = HGRN_CHUNK, HGRN_SUB
    n_sub = ch // sub
    lane_c = lax.broadcasted_iota(jnp.int32, (sub, ch), 1)
    row_c = lax.broadcasted_iota(jnp.int32, (sub, ch), 0)

    def chunk(ci):
        rows = slice(ci * ch, (ci + 1) * ch)
        hf = hf_ref[rows, :]
        hq = hq_ref[rows, :].astype(F32)
        v = hi_ref[rows, :].astype(F32)
        hg = hg_ref[rows, :].astype(F32)

        log_gate = log1m_lb + _log_sigmoid(hf)
        log_f = jnp.maximum(log_lb, log_gate) + jnp.log1p(jnp.exp(-jnp.abs(log_lb - log_gate)))
        k = (1.0 - lb) * _sigmoid(-hf)
        q = hq * _sigmoid(hq)
        b = _cumsum_rows(log_f)

        bref = jnp.concatenate(
            [jnp.broadcast_to(b[max(i * sub - 1, 0):max(i * sub - 1, 0) + 1, :], (sub, HGRN_DIM))
             for i in range(n_sub)], axis=0)
        q_rel = (q * jnp.exp(b - bref)).astype(BF16)

        a_rows = []
        for i in range(n_sub):
            s0 = i * sub
            bq = b[s0:s0 + sub, :]
            qq = q[s0:s0 + sub, :]
            diag = jnp.zeros((sub, ch), F32)
            for s in range(sub):
                bs = b[s0 + s:s0 + s + 1, :]
                ks = k[s0 + s:s0 + s + 1, :]
                e = jnp.exp(bq - bs)
                col = jnp.sum(qq * ks * e, axis=-1, keepdims=True)
                diag = jnp.where(lane_c == s0 + s, col, diag)
            diag = jnp.where(row_c + s0 >= lane_c, diag, 0.0)
            if i == 0:
                a_rows.append(diag)
            else:
                k_rel = (k * jnp.exp(jnp.minimum(bref[s0:s0 + 1, :] - b, 0.0))).astype(BF16)
                off = _dot_nt(q_rel[s0:s0 + sub, :], k_rel)
                a_rows.append(jnp.where(lane_c < s0, off, diag))
        a = jnp.concatenate(a_rows, axis=0).astype(BF16)

        state_t = state_ref[...]
        b_last = b[ch - 1:ch, :]
        out = _dot(a, v.astype(BF16)) + _dot_nt((q * jnp.exp(b)).astype(BF16), state_t.astype(BF16))
        k_end = (k * jnp.exp(b_last - b)).astype(BF16)
        state_ref[...] = state_t * jnp.exp(b_last) + _dot(v.T.astype(BF16), k_end)

        y = out * lax.rsqrt(jnp.mean(out * out, axis=-1, keepdims=True) + NORM_EPS) * gain_ref[...]
        o_ref[rows, :] = (y * (hg * _sigmoid(hg))).astype(o_ref.dtype)

    for ci in range(o_ref.shape[0] // ch):
        chunk(ci)


def hgrn2(proj, gates, lb_logits, gain, batch, layer, col_q, col_i, col_g):
    rows = proj.shape[0]
    seq = rows // batch
    tr = min(HGRN_ROWS, seq)
    nt = seq // tr
    n_layers = lb_logits.shape[0]

    def tok(col0):
        return pl.BlockSpec((tr, LANES), lambda b, h, t: (b * nt + t, col0 + h))

    return pl.pallas_call(
        functools.partial(_hgrn_kernel, layer=layer),
        grid=(batch, HGRN_HEADS, nt),
        in_specs=[tok(col_q), tok(col_i), tok(col_g), tok(0),
                  pl.BlockSpec((n_layers, LANES), lambda b, h, t: (0, h)),
                  pl.BlockSpec((1, LANES), lambda b, h, t: (0, h))],
        out_specs=tok(0),
        out_shape=jax.ShapeDtypeStruct((rows, HGRN_WIDTH), BF16),
        scratch_shapes=[pltpu.VMEM((HGRN_DIM, HGRN_DIM), F32)],
        compiler_params=_params("parallel", "parallel", "arbitrary"),
        name="hgrn2",
    )(proj, proj, proj, gates, lb_logits, gain.reshape(1, HGRN_WIDTH))


def _causal_conv_silu(x, tail_ref, w, bias):
    tail = tail_ref[...]
    row8 = lax.broadcasted_iota(jnp.int32, tail.shape, 0)
    acc = x * w[SSM_CONV - 1:SSM_CONV, :] + bias
    for j in range(1, SSM_CONV):
        rolled = pltpu.roll(x, j, axis=0)
        head = jnp.where(row8 < j, pltpu.roll(tail, j, axis=0), rolled[0:8, :])
        shifted = jnp.concatenate([head, rolled[8:, :]], axis=0)
        acc = acc + shifted * w[SSM_CONV - 1 - j:SSM_CONV - j, :]
    tail_ref[...] = x[x.shape[0] - 8:, :]
    return acc * _sigmoid(acc)


def _pair_select(first, a, b):
    return jnp.where(first, a, b)


def _ssd_kernel(z_ref, x_ref, bc_ref, dt_ref, cwx_ref, cwbc_ref, cbx_ref, cbbc_ref, dtb_ref, alog_ref,
                dskip_ref, gain_ref, o_ref, state_ref, tailx_ref, tailbc_ref):
    @pl.when(pl.program_id(1) == 0)
    def _():
        state_ref[...] = jnp.zeros(state_ref.shape, F32)
        tailx_ref[...] = jnp.zeros(tailx_ref.shape, F32)
        tailbc_ref[...] = jnp.zeros(tailbc_ref.shape, F32)

    ch = SSM_CHUNK
    dt = _softplus(dt_ref[...] + dtb_ref[...])
    a = -jnp.exp(alog_ref[...])
    cum = _cumsum_rows(dt * a)
    cum_t = cum.T
    dt_t = dt.T
    exp_cum = jnp.exp(cum)
    w_end = jnp.exp(cum[ch - 1:ch, :] - cum) * dt

    r = lax.broadcasted_iota(jnp.int32, (ch, ch), 0)
    c = lax.broadcasted_iota(jnp.int32, (ch, ch), 1)
    causal = r >= c
    first = lax.broadcasted_iota(jnp.int32, (ch, LANES), 1) < SSM_HEAD_DIM
    first_row = first[0:1, :]

    def bcast(col):
        return jnp.broadcast_to(col, (ch, LANES))

    for g in range(SSM_GROUPS):
        gx = slice(g * SSM_GROUP_WIDTH, (g + 1) * SSM_GROUP_WIDTH)
        gb = slice(g * SSM_STATE, (g + 1) * SSM_STATE)
        gc = slice(SSM_BC + g * SSM_STATE, SSM_BC + (g + 1) * SSM_STATE)
        xs = _causal_conv_silu(x_ref[:, gx].astype(F32), tailx_ref.at[:, gx], cwx_ref[:, gx], cbx_ref[:, gx])
        bm = _causal_conv_silu(bc_ref[:, gb].astype(F32), tailbc_ref.at[:, gb], cwbc_ref[:, gb], cbbc_ref[:, gb])
        cm = _causal_conv_silu(bc_ref[:, gc].astype(F32), tailbc_ref.at[:, gc], cwbc_ref[:, gc], cbbc_ref[:, gc])
        bm16, cm16 = bm.astype(BF16), cm.astype(BF16)
        cb = _dot_nt(cm16, bm16)
        state = state_ref[g]
        y_state = _dot(cm16, state.astype(BF16))

        y_pairs, wx_pairs, decay_pairs = [], [], []
        for pr in range(2):
            xp = xs[:, pr * LANES:(pr + 1) * LANES]
            xp16 = xp.astype(BF16)
            ys, scale_in, scale_w, ends = [], [], [], []
            for hh in range(2):
                h = g * 4 + pr * 2 + hh
                ccol = bcast(cum[:, h:h + 1])
                seg = ccol - cum_t[h:h + 1, :]
                decay = jnp.where(causal, jnp.exp(jnp.minimum(seg, 0.0)), 0.0)
                m = (cb * decay * dt_t[h:h + 1, :]).astype(BF16)
                ys.append(_dot(m, xp16))
                scale_in.append(bcast(exp_cum[:, h:h + 1]))
                scale_w.append(bcast(w_end[:, h:h + 1]))
                ends.append(ccol[ch - 1:ch, :])
            y_in = y_state[:, pr * LANES:(pr + 1) * LANES] * _pair_select(first, scale_in[0], scale_in[1])
            y_pairs.append(_pair_select(first, ys[0], ys[1]) + y_in)
            wx_pairs.append((xp * _pair_select(first, scale_w[0], scale_w[1])).astype(BF16))
            decay_pairs.append(jnp.exp(_pair_select(first_row, ends[0], ends[1])))
        y = jnp.concatenate(y_pairs, axis=-1)
        wx = jnp.concatenate(wx_pairs, axis=-1)
        state_ref[g] = state * jnp.concatenate(decay_pairs, axis=-1) + _dot(bm.T.astype(BF16), wx)

        y = y + dskip_ref[:, gx] * xs
        zg = z_ref[:, gx].astype(F32)
        y = y * (zg * _sigmoid(zg))
        y = y * lax.rsqrt(jnp.mean(y * y, axis=-1, keepdims=True) + NORM_EPS) * gain_ref[:, gx]
        o_ref[:, gx] = y.astype(o_ref.dtype)


def ssd_mixer(proj, dt_raw, conv_w, conv_b, dt_bias, a_log, d_skip, gain, batch):
    rows = proj.shape[0]
    seq = rows // batch
    ch = SSM_CHUNK
    nc = seq // ch

    def tok(width, col):
        return pl.BlockSpec((ch, width), lambda b, c: (b * nc + c, col))

    def par(nrows, width, col):
        return pl.BlockSpec((nrows, width), lambda b, c: (0, col))

    return pl.pallas_call(
        _ssd_kernel,
        grid=(batch, nc),
        in_specs=[tok(SSM_INNER, 0), tok(SSM_INNER, 1), tok(2 * SSM_BC, 2), tok(LANES, 0),
                  par(SSM_CONV, SSM_INNER, 0), par(SSM_CONV, 2 * SSM_BC, 1),
                  par(1, SSM_INNER, 0), par(1, 2 * SSM_BC, 1),
                  par(1, LANES, 0), par(1, LANES, 0), par(1, SSM_INNER, 0), par(1, SSM_INNER, 0)],
        out_specs=tok(SSM_INNER, 0),
        out_shape=jax.ShapeDtypeStruct((rows, SSM_INNER), BF16),
        scratch_shapes=[pltpu.VMEM((SSM_GROUPS, SSM_STATE, SSM_GROUP_WIDTH), F32),
                        pltpu.VMEM((8, SSM_INNER), F32), pltpu.VMEM((8, 2 * SSM_BC), F32)],
        compiler_params=_params("parallel", "arbitrary"),
        name="ssd_mixer",
    )(proj, proj, proj, dt_raw, conv_w, conv_w, conv_b, conv_b, dt_bias, a_log, d_skip, gain)


def _pad_lanes(x):
    return jnp.pad(x, ((0, 0), (0, LANES - x.shape[1])))


def _split_cols(w, sizes):
    out, lo = [], 0
    for s in sizes:
        out.append(w[:, lo:lo + s])
        lo += s
    return out


def kernel(x, mem, mem_norm, norm_mix, norm_xattn, norm_mlp, norm_final, ev_in_proj, fox_fgate_bias,
           hgrn_lb_logits, hgrn_out_norm, ev_out_proj, ssm_in_proj, ssm_conv_w, ssm_conv_b, ssm_dt_bias,
           ssm_A_log, ssm_D, ssm_norm, ssm_out_proj, xa_q, xa_kv, xa_o, mlp_up, mlp_down):
    batch, seq, d = x.shape
    depth = norm_mix.shape[0]
    h = x.reshape(batch * seq, d)
    mem2 = mem.reshape(-1, d)

    for layer in range(depth):
        if layer % 2 == 0:
            e = layer // 2
            fq, fk, fv, fg, hq, hf, hi, hg = _split_cols(
                ev_in_proj[e], (FOX_WIDTH, FOX_WIDTH, FOX_WIDTH, FOX_HEADS,
                                HGRN_WIDTH, HGRN_WIDTH, HGRN_WIDTH, HGRN_WIDTH))
            w_main = jnp.concatenate([fq * (FOX_HEAD_DIM ** -0.5 * LOG2E), fk, fv, hq, hi, hg], axis=1).astype(BF16)
            w_gate = jnp.concatenate([hf, _pad_lanes(fg)], axis=1).astype(BF16)
            proj = norm_matmul(h, norm_mix[layer], w_main, BF16, 512)
            gates = norm_matmul(h, norm_mix[layer], w_gate, F32, w_gate.shape[1])
            ccol, crow = fox_gates(gates, _pad_lanes(fox_fgate_bias[e].reshape(1, -1)), batch,
                                   HGRN_WIDTH // LANES)
            a_out = fox_attention(proj, ccol, crow, batch)
            blocks = FOX_WIDTH // LANES
            b_out = hgrn2(proj, gates, hgrn_lb_logits, hgrn_out_norm[e], batch, e,
                          3 * blocks, 3 * blocks + HGRN_HEADS, 3 * blocks + 2 * HGRN_HEADS)
            w_out = ev_out_proj[e].astype(BF16)
            h = matmul_residual([a_out, b_out], [w_out[:FOX_WIDTH], w_out[FOX_WIDTH:]], h)
        else:
            o = layer // 2
            w_in = ssm_in_proj[o]
            n_main = 2 * SSM_INNER + 2 * SSM_BC
            proj = norm_matmul(h, norm_mix[layer], w_in[:, :n_main].astype(BF16), BF16, 512)
            dt_raw = norm_matmul(h, norm_mix[layer], _pad_lanes(w_in[:, n_main:]).astype(BF16), F32, LANES)
            y = ssd_mixer(proj, dt_raw, ssm_conv_w[o], ssm_conv_b[o].reshape(1, -1),
                          _pad_lanes(ssm_dt_bias[o].reshape(1, -1)), _pad_lanes(ssm_A_log[o].reshape(1, -1)),
                          jnp.repeat(ssm_D[o], SSM_HEAD_DIM).reshape(1, -1), ssm_norm[o].reshape(1, -1), batch)
            h = matmul_residual([y], [ssm_out_proj[o].astype(BF16)], h)

        kv = norm_matmul(mem2, mem_norm, xa_kv[layer].astype(BF16), BF16, 512)
        h = cross_attention(h, norm_xattn[layer], xa_q[layer].astype(BF16), kv, xa_o[layer].astype(BF16), batch)
        last = layer == depth - 1
        h = mlp(h, norm_mlp[layer], mlp_up[layer].astype(BF16), mlp_down[layer].astype(BF16), norm_final, last)

    return h.reshape(batch, seq, d)
```

```python
import functools

import jax
import jax.numpy as jnp
from jax import lax
from jax.experimental import pallas as pl
from jax.experimental.pallas import tpu as pltpu

F32 = jnp.float32
BF16 = jnp.bfloat16
HIGHEST = lax.Precision.HIGHEST

NORM_EPS = 1e-6
MASK_VALUE = -1e30
LOG_FLOOR = 1e-30
LOG2E = 1.4426950408889634

LANES = 128
VMEM_LIMIT_BYTES = 56 * 1024 * 1024

FOX_HEADS = 8
FOX_HEAD_DIM = 64
FOX_WIDTH = FOX_HEADS * FOX_HEAD_DIM
FOX_PAIRS = FOX_WIDTH // LANES
FOX_TILE = 256
FOX_TQ = 512
FOX_TK = 256

HGRN_HEADS = 4
HGRN_DIM = 128
HGRN_WIDTH = HGRN_HEADS * HGRN_DIM
HGRN_CHUNK = 64
HGRN_SUB = 16
HGRN_ROWS = 256

SSM_INNER = 2048
SSM_HEAD_DIM = 64
SSM_HEADS = SSM_INNER // SSM_HEAD_DIM
SSM_GROUPS = 8
SSM_GROUP_WIDTH = SSM_INNER // SSM_GROUPS
SSM_STATE = 128
SSM_CONV = 4
SSM_CHUNK = 128
SSM_BC = SSM_GROUPS * SSM_STATE

XATTN_HEADS = 4
XATTN_HEAD_DIM = 128
XATTN_WIDTH = XATTN_HEADS * XATTN_HEAD_DIM

ROW_TILE = 512
MLP_ROW_TILE = 1024
MLP_FF_TILE = 512
EVEN_PROJ_TILE = 1536
SSM_PROJ_TILE = 2048


def _params(*semantics):
    return pltpu.CompilerParams(dimension_semantics=semantics, vmem_limit_bytes=VMEM_LIMIT_BYTES)


def _rms(x, gain):
    ms = jnp.mean(x * x, axis=-1, keepdims=True)
    return x * lax.rsqrt(ms + NORM_EPS) * gain


def _log_sigmoid(x):
    return jnp.minimum(x, 0.0) - jnp.log1p(jnp.exp(-jnp.abs(x)))


def _sigmoid(x):
    return 1.0 / (1.0 + jnp.exp(-x))


def _softplus(x):
    return jnp.maximum(x, 0.0) + jnp.log1p(jnp.exp(-jnp.abs(x)))


def _dot(a, b):
    return jnp.dot(a, b, preferred_element_type=F32)


def _dot_nt(a, b):
    return lax.dot_general(a, b, (((1,), (1,)), ((), ())), preferred_element_type=F32)


def _cumsum_rows(x):
    n = x.shape[0]
    r = lax.broadcasted_iota(jnp.int32, (n, n), 0)
    c = lax.broadcasted_iota(jnp.int32, (n, n), 1)
    tri = jnp.where(r >= c, 1.0, 0.0).astype(F32)
    return jnp.dot(tri, x, precision=HIGHEST, preferred_element_type=F32)


def _norm_matmul_kernel(h_ref, g_ref, w_ref, o_ref, hn_ref):
    @pl.when(pl.program_id(1) == 0)
    def _():
        hn_ref[...] = _rms(h_ref[...], g_ref[...]).astype(BF16)

    o_ref[...] = _dot(hn_ref[...], w_ref[...]).astype(o_ref.dtype)


def norm_matmul(h, gain, w, out_dtype, tn):
    rows, d = h.shape
    n = w.shape[1]
    tm = min(ROW_TILE * 2, rows)
    return pl.pallas_call(
        _norm_matmul_kernel,
        grid=(rows // tm, n // tn),
        in_specs=[pl.BlockSpec((tm, d), lambda i, j: (i, 0)),
                  pl.BlockSpec((1, d), lambda i, j: (0, 0)),
                  pl.BlockSpec((d, tn), lambda i, j: (0, j))],
        out_specs=pl.BlockSpec((tm, tn), lambda i, j: (i, j)),
        out_shape=jax.ShapeDtypeStruct((rows, n), out_dtype),
        scratch_shapes=[pltpu.VMEM((tm, d), BF16)],
        compiler_params=_params("parallel", "arbitrary"),
        name="norm_matmul",
    )(h, gain.reshape(1, d), w)


def _norm_proj_kernel(h_ref, g_ref, w_ref, wg_ref, o_ref, og_ref, hn_ref):
    @pl.when(pl.program_id(1) == 0)
    def _():
        hn = _rms(h_ref[...], g_ref[...]).astype(BF16)
        hn_ref[...] = hn
        og_ref[...] = _dot(hn, wg_ref[...])

    o_ref[...] = _dot(hn_ref[...], w_ref[...]).astype(o_ref.dtype)


def norm_proj(h, gain, w, w_gate, tn):
    rows, d = h.shape
    n, ng = w.shape[1], w_gate.shape[1]
    tm = min(ROW_TILE * 2, rows)
    return pl.pallas_call(
        _norm_proj_kernel,
        grid=(rows // tm, n // tn),
        in_specs=[pl.BlockSpec((tm, d), lambda i, j: (i, 0)),
                  pl.BlockSpec((1, d), lambda i, j: (0, 0)),
                  pl.BlockSpec((d, tn), lambda i, j: (0, j)),
                  pl.BlockSpec((d, ng), lambda i, j: (0, 0))],
        out_specs=[pl.BlockSpec((tm, tn), lambda i, j: (i, j)),
                   pl.BlockSpec((tm, ng), lambda i, j: (i, 0))],
        out_shape=[jax.ShapeDtypeStruct((rows, n), BF16), jax.ShapeDtypeStruct((rows, ng), F32)],
        scratch_shapes=[pltpu.VMEM((tm, d), BF16)],
        compiler_params=_params("parallel", "arbitrary"),
        name="norm_proj",
    )(h, gain.reshape(1, d), w, w_gate)


def _cross_attention(x, gain, wq_ref, kv_ref, wo_ref):
    hn = _rms(x, gain).astype(BF16)
    q = _dot(hn, wq_ref[...]).astype(BF16)
    scale = XATTN_HEAD_DIM ** -0.5
    outs = []
    for hd in range(XATTN_HEADS):
        lo = hd * XATTN_HEAD_DIM
        qh = q[:, lo:lo + XATTN_HEAD_DIM]
        kh = kv_ref[:, lo:lo + XATTN_HEAD_DIM]
        vh = kv_ref[:, XATTN_WIDTH + lo:XATTN_WIDTH + lo + XATTN_HEAD_DIM]
        s = _dot_nt(qh, kh) * scale
        p = jnp.exp(s - jnp.max(s, axis=-1, keepdims=True))
        denom = jnp.sum(p, axis=-1, keepdims=True)
        outs.append((_dot(p.astype(BF16), vh) / denom).astype(BF16))
    return _dot(jnp.concatenate(outs, axis=-1), wo_ref[...])


def _post_mixer_kernel(*refs, n_mix, final_norm):
    mix_refs, wout_refs = refs[:n_mix], refs[n_mix:2 * n_mix]
    (h_ref, gx_ref, wq_ref, kv_ref, wo_ref, gm_ref, wu_ref, wd_ref, gf_ref,
     o_ref, hn_ref) = refs[2 * n_mix:]
    f = pl.program_id(1)

    @pl.when(f == 0)
    def _():
        x = h_ref[...]
        for m_ref, w_ref in zip(mix_refs, wout_refs):
            x = x + _dot(m_ref[...], w_ref[...])
        x = x + _cross_attention(x, gx_ref[...], wq_ref, kv_ref, wo_ref)
        hn_ref[...] = _rms(x, gm_ref[...]).astype(BF16)
        o_ref[...] = x

    u = _dot(hn_ref[...], wu_ref[...])
    u = jnp.square(jnp.maximum(u, 0.0)).astype(BF16)
    o_ref[...] += _dot(u, wd_ref[...])

    if final_norm:
        @pl.when(f == pl.num_programs(1) - 1)
        def _():
            o_ref[...] = _rms(o_ref[...], gf_ref[...])


def post_mixer(h, mixes, w_outs, gain_x, wq, kv, wo, gain_mlp, w_up, w_down, final_gain, final_norm, batch):
    rows, d = h.shape
    seq = rows // batch
    mem_len = kv.shape[0] // batch
    ff = w_up.shape[1]
    tm = min(MLP_ROW_TILE, seq)
    tf = min(MLP_FF_TILE, ff)
    tiles_per_batch = seq // tm

    def const(shape):
        return pl.BlockSpec(shape, lambda i, f: (0, 0))

    in_specs = [pl.BlockSpec((tm, m.shape[1]), lambda i, f: (i, 0)) for m in mixes]
    in_specs += [const(w.shape) for w in w_outs]
    in_specs += [pl.BlockSpec((tm, d), lambda i, f: (i, 0)), const((1, d)), const(wq.shape),
                 pl.BlockSpec((mem_len, 2 * XATTN_WIDTH), lambda i, f: (i // tiles_per_batch, 0)),
                 const(wo.shape), const((1, d)),
                 pl.BlockSpec((d, tf), lambda i, f: (0, f)),
                 pl.BlockSpec((tf, d), lambda i, f: (f, 0)),
                 const((1, d))]
    return pl.pallas_call(
        functools.partial(_post_mixer_kernel, n_mix=len(mixes), final_norm=final_norm),
        grid=(rows // tm, ff // tf),
        in_specs=in_specs,
        out_specs=pl.BlockSpec((tm, d), lambda i, f: (i, 0)),
        out_shape=jax.ShapeDtypeStruct((rows, d), F32),
        scratch_shapes=[pltpu.VMEM((tm, d), BF16)],
        compiler_params=_params("parallel", "arbitrary"),
        name="post_mixer",
    )(*mixes, *w_outs, h, gain_x.reshape(1, d), wq, kv, wo, gain_mlp.reshape(1, d), w_up, w_down,
      final_gain.reshape(1, d))


def _fox_gate_kernel(fg_ref, bias_ref, ccol_ref, crow_ref):
    seq = fg_ref.shape[0]
    blk = FOX_TILE
    carry = jnp.zeros((1, LANES), F32)
    for i in range(seq // blk):
        rows = slice(i * blk, (i + 1) * blk)
        c = _cumsum_rows(_log_sigmoid(fg_ref[rows, :] + bias_ref[...])) + carry
        c2 = c * LOG2E
        ccol_ref[rows, :] = c2
        crow_ref[0, :, rows] = c2.T[:FOX_HEADS, :]
        carry = c[blk - 1:blk, :]


def fox_gates(gates, bias, batch, col_block):
    rows = gates.shape[0]
    seq = rows // batch
    return pl.pallas_call(
        _fox_gate_kernel,
        grid=(batch,),
        in_specs=[pl.BlockSpec((seq, LANES), lambda b: (b, col_block)),
                  pl.BlockSpec((1, LANES), lambda b: (0, 0))],
        out_specs=[pl.BlockSpec((seq, LANES), lambda b: (b, 0)),
                   pl.BlockSpec((1, FOX_HEADS, seq), lambda b: (b, 0, 0))],
        out_shape=[jax.ShapeDtypeStruct((rows, LANES), F32),
                   jax.ShapeDtypeStruct((batch, FOX_HEADS, seq), F32)],
        compiler_params=_params("parallel"),
        name="fox_gates",
    )(gates, bias)


FOX_ONES_ROWS = 16


def _fox_kernel(q_ref, k_ref, v_ref, ccol_ref, crow_ref, o_ref, vt_ref, cs_ref, st_ref, p_ref, alpha_ref,
                acc_ref, m_ref):
    pair = pl.program_id(1)
    qi = pl.program_id(2)
    tq, tk = FOX_TQ, FOX_TK
    seq = k_ref.shape[0]
    hd = FOX_HEAD_DIM

    @pl.when(qi == 0)
    def _():
        lane = lax.broadcasted_iota(jnp.int32, (tk, LANES), 1)
        for blk in range(seq // tk):
            rows = slice(blk * tk, (blk + 1) * tk)
            vt = v_ref[rows, :].astype(F32).T
            c = ccol_ref[rows, :]
            for hh in range(2):
                vt_ref[hh, 0:hd, rows] = vt[hh * hd:(hh + 1) * hd, :].astype(BF16)
                vt_ref[hh, hd:hd + FOX_ONES_ROWS, rows] = jnp.ones((FOX_ONES_ROWS, tk), BF16)
                col = jnp.sum(jnp.where(lane == 2 * pair + hh, c, 0.0), axis=-1, keepdims=True)
                cs_ref[hh, rows, :] = jnp.broadcast_to(col, (tk, LANES))

    n_kv = (qi + 1) * (tq // tk)
    q_start = qi * tq
    qt = q_ref[...].astype(F32).T
    row = lax.broadcasted_iota(jnp.int32, (LANES, tq), 0)
    qts = (jnp.where(row < hd, qt, 0.0).astype(BF16), jnp.where(row < hd, 0.0, qt).astype(BF16))
    cts = [crow_ref[0, pl.ds(2 * pair + hh, 1), pl.ds(pl.multiple_of(q_start, tq), tq)]
           for hh in range(2)]
    key_minus_query = (lax.broadcasted_iota(jnp.int32, (tk, tq), 0)
                       - lax.broadcasted_iota(jnp.int32, (tk, tq), 1))

    def scores(j):
        k = k_ref[pl.ds(pl.multiple_of(j * tk, tk), tk), :]
        return [_dot(k, qts[hh]) for hh in range(2)]

    def values_times_probs(j, par):
        start = pl.multiple_of(j * tk, tk)
        return [_dot(vt_ref[hh, :, pl.ds(start, tk)], p_ref[par, hh]) for hh in range(2)]

    m_ref[...] = jnp.full(m_ref.shape, MASK_VALUE, F32)
    acc_ref[...] = jnp.zeros(acc_ref.shape, F32)
    p_ref[1] = jnp.zeros(p_ref.shape[1:], BF16)
    alpha_ref[1] = jnp.ones(alpha_ref.shape[1:], F32)
    first_scores = scores(0)
    for hh in range(2):
        st_ref[0, hh] = first_scores[hh]

    def step(j, masked, has_next):
        par = lax.rem(j, 2)
        prev = 1 - par
        pv = values_times_probs(jnp.maximum(j - 1, 0), prev)
        if has_next:
            next_scores = scores(j + 1)
        start = pl.multiple_of(j * tk, tk)
        for hh in range(2):
            cs = cs_ref[hh, pl.ds(start, tk), :]
            st = st_ref[par, hh] + (cts[hh] - jnp.concatenate([cs] * (tq // LANES), axis=1))
            if masked:
                st = jnp.where(key_minus_query <= q_start - start, st, MASK_VALUE)
            m_old = m_ref[hh]
            m_new = jnp.maximum(m_old, jnp.max(st, axis=0, keepdims=True))
            p_ref[par, hh] = jnp.exp2(st - m_new).astype(BF16)
            acc_ref[hh] = alpha_ref[prev, hh] * acc_ref[hh] + pv[hh]
            alpha_ref[par, hh] = jnp.exp2(m_old - m_new)
            m_ref[hh] = m_new
        if has_next:
            for hh in range(2):
                st_ref[prev, hh] = next_scores[hh]

    def body(j, carry):
        step(j, masked=False, has_next=True)
        return carry

    n_diag = tq // tk
    lax.fori_loop(0, n_kv - n_diag, body, 0)
    for d in range(n_diag):
        step(n_kv - n_diag + d, masked=True, has_next=d < n_diag - 1)
    last = n_kv - 1
    last_par = lax.rem(last, 2)
    pv = values_times_probs(last, last_par)
    outs = []
    for hh in range(2):
        acc = alpha_ref[last_par, hh] * acc_ref[hh] + pv[hh]
        outs.append(acc[0:hd, :] / acc[hd:hd + 1, :])
    o_ref[...] = jnp.concatenate(outs, axis=0).T.astype(o_ref.dtype)


def fox_attention(proj, ccol, crow, batch):
    rows = proj.shape[0]
    seq = rows // batch
    tq, tk = FOX_TQ, FOX_TK
    nq = seq // tq
    acc_rows = FOX_HEAD_DIM + FOX_ONES_ROWS
    return pl.pallas_call(
        _fox_kernel,
        grid=(batch, FOX_PAIRS, nq),
        in_specs=[pl.BlockSpec((tq, LANES), lambda b, p, i: (b * nq + i, p)),
                  pl.BlockSpec((seq, LANES), lambda b, p, i: (b, FOX_PAIRS + p)),
                  pl.BlockSpec((seq, LANES), lambda b, p, i: (b, 2 * FOX_PAIRS + p)),
                  pl.BlockSpec((seq, LANES), lambda b, p, i: (b, 0)),
                  pl.BlockSpec((1, FOX_HEADS, seq), lambda b, p, i: (b, 0, 0))],
        out_specs=pl.BlockSpec((tq, LANES), lambda b, p, i: (b * nq + i, p)),
        out_shape=jax.ShapeDtypeStruct((rows, FOX_WIDTH), BF16),
        scratch_shapes=[pltpu.VMEM((2, acc_rows, seq), BF16), pltpu.VMEM((2, seq, LANES), F32),
                        pltpu.VMEM((2, 2, tk, tq), F32), pltpu.VMEM((2, 2, tk, tq), BF16),
                        pltpu.VMEM((2, 2, 1, tq), F32),
                        pltpu.VMEM((2, acc_rows, tq), F32), pltpu.VMEM((2, 1, tq), F32)],
        compiler_params=_params("parallel", "parallel", "arbitrary"),
        name="fox_attention",
    )(proj, proj, proj, ccol, crow)


def _hgrn_kernel(hq_ref, hi_ref, hg_ref, hf_ref, lbl_ref, gain_ref, o_ref, state_ref, *, layer):
    @pl.when(pl.program_id(2) == 0)
    def _():
        state_ref[...] = jnp.zeros(state_ref.shape, F32)

    logits = lbl_ref[...]
    w = jnp.exp(logits - jnp.max(logits, axis=0, keepdims=True))
    w = w / jnp.sum(w, axis=0, keepdims=True)
    lb = jnp.sum(w[:layer + 1], axis=0, keepdims=True) - w[0:1]
    log_lb = jnp.log(jnp.maximum(lb, LOG_FLOOR))
    log1m_lb = jnp.log1p(-lb)

    ch, sub = HGRN_CHUNK, HGRN_SUB
    n_sub = ch // sub
    lane_c = lax.broadcasted_iota(jnp.int32, (sub, ch), 1)
    row_c = lax.broadcasted_iota(jnp.int32, (sub, ch), 0)

    def chunk(ci):
        rows = slice(ci * ch, (ci + 1) * ch)
        hf = hf_ref[rows, :]
        hq = hq_ref[rows, :].astype(F32)
        v = hi_ref[rows, :].astype(F32)
        hg = hg_ref[rows, :].astype(F32)

        log_gate = log1m_lb + _log_sigmoid(hf)
        log_f = jnp.maximum(log_lb, log_gate) + jnp.log1p(jnp.exp(-jnp.abs(log_lb - log_gate)))
        k = (1.0 - lb) * _sigmoid(-hf)
        q = hq * _sigmoid(hq)
        b = _cumsum_rows(log_f) * LOG2E

        bref = jnp.concatenate(
            [jnp.broadcast_to(b[max(i * sub - 1, 0):max(i * sub - 1, 0) + 1, :], (sub, HGRN_DIM))
             for i in range(n_sub)], axis=0)
        q_rel = (q * jnp.exp2(b - bref)).astype(BF16)

        a_rows = []
        for i in range(n_sub):
            s0 = i * sub
            bq = b[s0:s0 + sub, :]
            qq = q[s0:s0 + sub, :]
            diag = jnp.zeros((sub, ch), F32)
            for s in range(sub):
                bs = b[s0 + s:s0 + s + 1, :]
                ks = k[s0 + s:s0 + s + 1, :]
                e = jnp.exp2(bq - bs)
                col = jnp.sum(qq * ks * e, axis=-1, keepdims=True)
                diag = jnp.where(lane_c == s0 + s, col, diag)
            diag = jnp.where(row_c + s0 >= lane_c, diag, 0.0)
            if i == 0:
                a_rows.append(diag)
            else:
                k_rel = (k * jnp.exp2(jnp.minimum(bref[s0:s0 + 1, :] - b, 0.0))).astype(BF16)
                off = _dot_nt(q_rel[s0:s0 + sub, :], k_rel)
                a_rows.append(jnp.where(lane_c < s0, off, diag))
        a = jnp.concatenate(a_rows, axis=0).astype(BF16)

        state_t = state_ref[...]
        b_last = b[ch - 1:ch, :]
        out = _dot(a, v.astype(BF16)) + _dot_nt((q * jnp.exp2(b)).astype(BF16), state_t.astype(BF16))
        k_end = (k * jnp.exp2(b_last - b)).astype(BF16)
        state_ref[...] = state_t * jnp.exp2(b_last) + _dot(v.T.astype(BF16), k_end)

        y = out * lax.rsqrt(jnp.mean(out * out, axis=-1, keepdims=True) + NORM_EPS) * gain_ref[...]
        o_ref[rows, :] = (y * (hg * _sigmoid(hg))).astype(o_ref.dtype)

    for ci in range(o_ref.shape[0] // ch):
        chunk(ci)


def hgrn2(proj, gates, lb_logits, gain, batch, layer, col_q, col_i, col_g):
    rows = proj.shape[0]
    seq = rows // batch
    tr = min(HGRN_ROWS, seq)
    nt = seq // tr
    n_layers = lb_logits.shape[0]

    def tok(col0):
        return pl.BlockSpec((tr, LANES), lambda b, h, t: (b * nt + t, col0 + h))

    return pl.pallas_call(
        functools.partial(_hgrn_kernel, layer=layer),
        grid=(batch, HGRN_HEADS, nt),
        in_specs=[tok(col_q), tok(col_i), tok(col_g), tok(0),
                  pl.BlockSpec((n_layers, LANES), lambda b, h, t: (0, h)),
                  pl.BlockSpec((1, LANES), lambda b, h, t: (0, h))],
        out_specs=tok(0),
        out_shape=jax.ShapeDtypeStruct((rows, HGRN_WIDTH), BF16),
        scratch_shapes=[pltpu.VMEM((HGRN_DIM, HGRN_DIM), F32)],
        compiler_params=_params("parallel", "parallel", "arbitrary"),
        name="hgrn2",
    )(proj, proj, proj, gates, lb_logits, gain.reshape(1, HGRN_WIDTH))


def _causal_conv_silu(x, tail_ref, w, bias):
    n_blocks = x.shape[0] // 8
    tail = tail_ref[...]
    row8 = lax.broadcasted_iota(jnp.int32, tail.shape, 0)
    blocks = [tail] + [x[8 * i:8 * i + 8, :] for i in range(n_blocks)]
    acc = x * w[SSM_CONV - 1:SSM_CONV, :] + bias
    for j in range(1, SSM_CONV):
        rolled = [pltpu.roll(blk, j, axis=0) for blk in blocks]
        shifted = jnp.concatenate(
            [jnp.where(row8 < j, rolled[i], rolled[i + 1]) for i in range(n_blocks)], axis=0)
        acc = acc + shifted * w[SSM_CONV - 1 - j:SSM_CONV - j, :]
    tail_ref[...] = blocks[-1]
    return acc * _sigmoid(acc)


def _pair_select(first, a, b):
    return jnp.where(first, a, b)


def _ssd_kernel(z_ref, x_ref, bc_ref, dt_ref, cwx_ref, cwbc_ref, cbx_ref, cbbc_ref, dtb_ref, alog_ref,
                dskip_ref, gain_ref, o_ref, state_ref, tailx_ref, tailbc_ref):
    @pl.when(pl.program_id(1) == 0)
    def _():
        state_ref[...] = jnp.zeros(state_ref.shape, F32)
        tailx_ref[...] = jnp.zeros(tailx_ref.shape, F32)
        tailbc_ref[...] = jnp.zeros(tailbc_ref.shape, F32)

    ch = SSM_CHUNK
    dt = _softplus(dt_ref[...] + dtb_ref[...])
    a = -jnp.exp(alog_ref[...])
    cum = _cumsum_rows(dt * a)
    cum_t = cum.T
    dt_t = dt.T
    exp_cum = jnp.exp(cum)
    w_end = jnp.exp(cum[ch - 1:ch, :] - cum) * dt

    r = lax.broadcasted_iota(jnp.int32, (ch, ch), 0)
    c = lax.broadcasted_iota(jnp.int32, (ch, ch), 1)
    causal = r >= c
    first = lax.broadcasted_iota(jnp.int32, (ch, LANES), 1) < SSM_HEAD_DIM
    first_row = first[0:1, :]

    def bcast(col):
        return jnp.broadcast_to(col, (ch, LANES))

    for g in range(SSM_GROUPS):
        gx = slice(g * SSM_GROUP_WIDTH, (g + 1) * SSM_GROUP_WIDTH)
        gb = slice(g * SSM_STATE, (g + 1) * SSM_STATE)
        gc = slice(SSM_BC + g * SSM_STATE, SSM_BC + (g + 1) * SSM_STATE)
        xs = _causal_conv_silu(x_ref[:, gx].astype(F32), tailx_ref.at[:, gx], cwx_ref[:, gx], cbx_ref[:, gx])
        bm = _causal_conv_silu(bc_ref[:, gb].astype(F32), tailbc_ref.at[:, gb], cwbc_ref[:, gb], cbbc_ref[:, gb])
        cm = _causal_conv_silu(bc_ref[:, gc].astype(F32), tailbc_ref.at[:, gc], cwbc_ref[:, gc], cbbc_ref[:, gc])
        bm16, cm16 = bm.astype(BF16), cm.astype(BF16)
        cb = _dot_nt(cm16, bm16)
        state = state_ref[g]
        y_state = _dot(cm16, state.astype(BF16))

        y_pairs, wx_pairs, decay_pairs = [], [], []
        for pr in range(2):
            xp = xs[:, pr * LANES:(pr + 1) * LANES]
            xp16 = xp.astype(BF16)
            ys, scale_in, scale_w, ends = [], [], [], []
            for hh in range(2):
                h = g * 4 + pr * 2 + hh
                ccol = bcast(cum[:, h:h + 1])
                seg = ccol - cum_t[h:h + 1, :]
                decay = jnp.where(causal, jnp.exp(jnp.minimum(seg, 0.0)), 0.0)
                m = (cb * decay * dt_t[h:h + 1, :]).astype(BF16)
                ys.append(_dot(m, xp16))
                scale_in.append(bcast(exp_cum[:, h:h + 1]))
                scale_w.append(bcast(w_end[:, h:h + 1]))
                ends.append(ccol[ch - 1:ch, :])
            y_in = y_state[:, pr * LANES:(pr + 1) * LANES] * _pair_select(first, scale_in[0], scale_in[1])
            y_pairs.append(_pair_select(first, ys[0], ys[1]) + y_in)
            wx_pairs.append((xp * _pair_select(first, scale_w[0], scale_w[1])).astype(BF16))
            decay_pairs.append(jnp.exp(_pair_select(first_row, ends[0], ends[1])))
        y = jnp.concatenate(y_pairs, axis=-1)
        wx = jnp.concatenate(wx_pairs, axis=-1)
        state_ref[g] = state * jnp.concatenate(decay_pairs, axis=-1) + _dot(bm.T.astype(BF16), wx)

        y = y + dskip_ref[:, gx] * xs
        zg = z_ref[:, gx].astype(F32)
        y = y * (zg * _sigmoid(zg))
        y = y * lax.rsqrt(jnp.mean(y * y, axis=-1, keepdims=True) + NORM_EPS) * gain_ref[:, gx]
        o_ref[:, gx] = y.astype(o_ref.dtype)


def ssd_mixer(proj, dt_raw, conv_w, conv_b, dt_bias, a_log, d_skip, gain, batch):
    rows = proj.shape[0]
    seq = rows // batch
    ch = SSM_CHUNK
    nc = seq // ch

    def tok(width, col):
        return pl.BlockSpec((ch, width), lambda b, c: (b * nc + c, col))

    def par(nrows, width, col):
        return pl.BlockSpec((nrows, width), lambda b, c: (0, col))

    return pl.pallas_call(
        _ssd_kernel,
        grid=(batch, nc),
        in_specs=[tok(SSM_INNER, 0), tok(SSM_INNER, 1), tok(2 * SSM_BC, 2), tok(LANES, 0),
                  par(SSM_CONV, SSM_INNER, 0), par(SSM_CONV, 2 * SSM_BC, 1),
                  par(1, SSM_INNER, 0), par(1, 2 * SSM_BC, 1),
                  par(1, LANES, 0), par(1, LANES, 0), par(1, SSM_INNER, 0), par(1, SSM_INNER, 0)],
        out_specs=tok(SSM_INNER, 0),
        out_shape=jax.ShapeDtypeStruct((rows, SSM_INNER), BF16),
        scratch_shapes=[pltpu.VMEM((SSM_GROUPS, SSM_STATE, SSM_GROUP_WIDTH), F32),
                        pltpu.VMEM((8, SSM_INNER), F32), pltpu.VMEM((8, 2 * SSM_BC), F32)],
        compiler_params=_params("parallel", "arbitrary"),
        name="ssd_mixer",
    )(proj, proj, proj, dt_raw, conv_w, conv_w, conv_b, conv_b, dt_bias, a_log, d_skip, gain)


def _pad_lanes(x):
    return jnp.pad(x, ((0, 0), (0, LANES - x.shape[1])))


def _split_cols(w, sizes):
    out, lo = [], 0
    for s in sizes:
        out.append(w[:, lo:lo + s])
        lo += s
    return out


def kernel(x, mem, mem_norm, norm_mix, norm_xattn, norm_mlp, norm_final, ev_in_proj, fox_fgate_bias,
           hgrn_lb_logits, hgrn_out_norm, ev_out_proj, ssm_in_proj, ssm_conv_w, ssm_conv_b, ssm_dt_bias,
           ssm_A_log, ssm_D, ssm_norm, ssm_out_proj, xa_q, xa_kv, xa_o, mlp_up, mlp_down):
    batch, seq, d = x.shape
    depth = norm_mix.shape[0]
    h = x.reshape(batch * seq, d)
    mem2 = mem.reshape(-1, d)

    for layer in range(depth):
        if layer % 2 == 0:
            e = layer // 2
            fq, fk, fv, fg, hq, hf, hi, hg = _split_cols(
                ev_in_proj[e], (FOX_WIDTH, FOX_WIDTH, FOX_WIDTH, FOX_HEADS,
                                HGRN_WIDTH, HGRN_WIDTH, HGRN_WIDTH, HGRN_WIDTH))
            w_main = jnp.concatenate([fq * (FOX_HEAD_DIM ** -0.5 * LOG2E), fk, fv, hq, hi, hg], axis=1).astype(BF16)
            w_gate = jnp.concatenate([hf, _pad_lanes(fg)], axis=1).astype(BF16)
            proj, gates = norm_proj(h, norm_mix[layer], w_main, w_gate, EVEN_PROJ_TILE)
            ccol, crow = fox_gates(gates, _pad_lanes(fox_fgate_bias[e].reshape(1, -1)), batch,
                                   HGRN_WIDTH // LANES)
            a_out = fox_attention(proj, ccol, crow, batch)
            blocks = FOX_WIDTH // LANES
            b_out = hgrn2(proj, gates, hgrn_lb_logits, hgrn_out_norm[e], batch, e,
                          3 * blocks, 3 * blocks + HGRN_HEADS, 3 * blocks + 2 * HGRN_HEADS)
            w_out = ev_out_proj[e].astype(BF16)
            mixes, w_outs = [a_out, b_out], [w_out[:FOX_WIDTH], w_out[FOX_WIDTH:]]
        else:
            o = layer // 2
            w_in = ssm_in_proj[o]
            n_main = 2 * SSM_INNER + 2 * SSM_BC
            proj, dt_raw = norm_proj(h, norm_mix[layer], w_in[:, :n_main].astype(BF16),
                                     _pad_lanes(w_in[:, n_main:]).astype(BF16), SSM_PROJ_TILE)
            y = ssd_mixer(proj, dt_raw, ssm_conv_w[o], ssm_conv_b[o].reshape(1, -1),
                          _pad_lanes(ssm_dt_bias[o].reshape(1, -1)), _pad_lanes(ssm_A_log[o].reshape(1, -1)),
                          jnp.repeat(ssm_D[o], SSM_HEAD_DIM).reshape(1, -1), ssm_norm[o].reshape(1, -1), batch)
            mixes, w_outs = [y], [ssm_out_proj[o].astype(BF16)]

        kv = norm_matmul(mem2, mem_norm, xa_kv[layer].astype(BF16), BF16, 2 * XATTN_WIDTH)
        h = post_mixer(h, mixes, w_outs, norm_xattn[layer], xa_q[layer].astype(BF16), kv,
                       xa_o[layer].astype(BF16), norm_mlp[layer], mlp_up[layer].astype(BF16),
                       mlp_down[layer].astype(BF16), norm_final, layer == depth - 1, batch)

    return h.reshape(batch, seq, d)
```

```python
import functools

import jax
import jax.numpy as jnp
from jax import lax
from jax.experimental import pallas as pl
from jax.experimental.pallas import tpu as pltpu

F32 = jnp.float32
BF16 = jnp.bfloat16
HIGHEST = lax.Precision.HIGHEST

NORM_EPS = 1e-6
MASK_VALUE = -1e30
LOG_FLOOR = 1e-30
LOG2E = 1.4426950408889634

LANES = 128
VMEM_LIMIT_BYTES = 56 * 1024 * 1024

FOX_HEADS = 8
FOX_HEAD_DIM = 64
FOX_WIDTH = FOX_HEADS * FOX_HEAD_DIM
FOX_PAIRS = FOX_WIDTH // LANES
FOX_TILE = 256
FOX_TQ = 512
FOX_TK = 256

HGRN_HEADS = 4
HGRN_DIM = 128
HGRN_WIDTH = HGRN_HEADS * HGRN_DIM
HGRN_CHUNK = 64
HGRN_SUB = 16
HGRN_ROWS = 1024

SSM_INNER = 2048
SSM_HEAD_DIM = 64
SSM_HEADS = SSM_INNER // SSM_HEAD_DIM
SSM_GROUPS = 8
SSM_GROUP_WIDTH = SSM_INNER // SSM_GROUPS
SSM_STATE = 128
SSM_CONV = 4
SSM_CHUNK = 128
SSM_ROWS = 256
SSM_BC = SSM_GROUPS * SSM_STATE

XATTN_HEADS = 4
XATTN_HEAD_DIM = 128
XATTN_WIDTH = XATTN_HEADS * XATTN_HEAD_DIM

ROW_TILE = 512
MLP_ROW_TILE = 1024
MLP_FF_TILE = 1024
EVEN_PROJ_TILE = 1536
SSM_PROJ_TILE = 2048


def _params(*semantics):
    return pltpu.CompilerParams(dimension_semantics=semantics, vmem_limit_bytes=VMEM_LIMIT_BYTES)


def _rms(x, gain):
    ms = jnp.mean(x * x, axis=-1, keepdims=True)
    return x * lax.rsqrt(ms + NORM_EPS) * gain


def _log_sigmoid(x):
    return jnp.minimum(x, 0.0) - jnp.log1p(jnp.exp(-jnp.abs(x)))


def _sigmoid(x):
    return 1.0 / (1.0 + jnp.exp(-x))


def _softplus(x):
    return jnp.maximum(x, 0.0) + jnp.log1p(jnp.exp(-jnp.abs(x)))


def _dot(a, b):
    return jnp.dot(a, b, preferred_element_type=F32)


def _dot_nt(a, b):
    return lax.dot_general(a, b, (((1,), (1,)), ((), ())), preferred_element_type=F32)


def _cumsum_rows(x):
    n = x.shape[0]
    r = lax.broadcasted_iota(jnp.int32, (n, n), 0)
    c = lax.broadcasted_iota(jnp.int32, (n, n), 1)
    tri = jnp.where(r >= c, 1.0, 0.0).astype(F32)
    return jnp.dot(tri, x, precision=HIGHEST, preferred_element_type=F32)


def _norm_matmul_kernel(h_ref, g_ref, w_ref, o_ref, hn_ref):
    @pl.when(pl.program_id(1) == 0)
    def _():
        hn_ref[...] = _rms(h_ref[...], g_ref[...]).astype(BF16)

    o_ref[...] = _dot(hn_ref[...], w_ref[...]).astype(o_ref.dtype)


def norm_matmul(h, gain, w, out_dtype, tn):
    rows, d = h.shape
    n = w.shape[1]
    tm = min(ROW_TILE * 2, rows)
    return pl.pallas_call(
        _norm_matmul_kernel,
        grid=(rows // tm, n // tn),
        in_specs=[pl.BlockSpec((tm, d), lambda i, j: (i, 0)),
                  pl.BlockSpec((1, d), lambda i, j: (0, 0)),
                  pl.BlockSpec((d, tn), lambda i, j: (0, j))],
        out_specs=pl.BlockSpec((tm, tn), lambda i, j: (i, j)),
        out_shape=jax.ShapeDtypeStruct((rows, n), out_dtype),
        scratch_shapes=[pltpu.VMEM((tm, d), BF16)],
        compiler_params=_params("parallel", "arbitrary"),
        name="norm_matmul",
    )(h, gain.reshape(1, d), w)


def _norm_proj_kernel(h_ref, g_ref, w_ref, wg_ref, o_ref, og_ref, hn_ref):
    @pl.when(pl.program_id(1) == 0)
    def _():
        hn = _rms(h_ref[...], g_ref[...]).astype(BF16)
        hn_ref[...] = hn
        og_ref[...] = _dot(hn, wg_ref[...])

    o_ref[...] = _dot(hn_ref[...], w_ref[...]).astype(o_ref.dtype)


def norm_proj(h, gain, w, w_gate, tn):
    rows, d = h.shape
    n, ng = w.shape[1], w_gate.shape[1]
    tm = min(ROW_TILE * 2, rows)
    return pl.pallas_call(
        _norm_proj_kernel,
        grid=(rows // tm, n // tn),
        in_specs=[pl.BlockSpec((tm, d), lambda i, j: (i, 0)),
                  pl.BlockSpec((1, d), lambda i, j: (0, 0)),
                  pl.BlockSpec((d, tn), lambda i, j: (0, j)),
                  pl.BlockSpec((d, ng), lambda i, j: (0, 0))],
        out_specs=[pl.BlockSpec((tm, tn), lambda i, j: (i, j)),
                   pl.BlockSpec((tm, ng), lambda i, j: (i, 0))],
        out_shape=[jax.ShapeDtypeStruct((rows, n), BF16), jax.ShapeDtypeStruct((rows, ng), F32)],
        scratch_shapes=[pltpu.VMEM((tm, d), BF16)],
        compiler_params=_params("parallel", "arbitrary"),
        name="norm_proj",
    )(h, gain.reshape(1, d), w, w_gate)


def _causal_conv_silu(x, tail, w, bias):
    row8 = lax.broadcasted_iota(jnp.int32, tail.shape, 0)
    acc = x * w[SSM_CONV - 1:SSM_CONV, :] + bias
    for j in range(1, SSM_CONV):
        rolled = pltpu.roll(x, j, axis=0)
        head = jnp.where(row8 < j, pltpu.roll(tail, j, axis=0), rolled[0:8, :])
        shifted = jnp.concatenate([head, rolled[8:, :]], axis=0)
        acc = acc + shifted * w[SSM_CONV - 1 - j:SSM_CONV - j, :]
    return acc * _sigmoid(acc)


def _cross_attention(x, gain, wq_ref, kv_ref, wo_ref):
    hn = _rms(x, gain).astype(BF16)
    q = _dot(hn, wq_ref[...]).astype(BF16)
    scale = XATTN_HEAD_DIM ** -0.5
    outs = []
    for hd in range(XATTN_HEADS):
        lo = hd * XATTN_HEAD_DIM
        qh = q[:, lo:lo + XATTN_HEAD_DIM]
        kh = kv_ref[:, lo:lo + XATTN_HEAD_DIM]
        vh = kv_ref[:, XATTN_WIDTH + lo:XATTN_WIDTH + lo + XATTN_HEAD_DIM]
        s = _dot_nt(qh, kh) * scale
        p = jnp.exp(s - jnp.max(s, axis=-1, keepdims=True))
        denom = jnp.sum(p, axis=-1, keepdims=True)
        outs.append((_dot(p.astype(BF16), vh) / denom).astype(BF16))
    return _dot(jnp.concatenate(outs, axis=-1), wo_ref[...])


def _post_mixer_kernel(*refs, n_mix, final_norm):
    mix_refs, wout_refs = refs[:n_mix], refs[n_mix:2 * n_mix]
    (h_ref, gx_ref, wq_ref, kv_ref, wo_ref, gm_ref, wu_ref, wd_ref, gf_ref,
     o_ref, hn_ref) = refs[2 * n_mix:]
    f = pl.program_id(1)

    @pl.when(f == 0)
    def _():
        x = h_ref[...]
        for m_ref, w_ref in zip(mix_refs, wout_refs):
            x = x + _dot(m_ref[...], w_ref[...])
        x = x + _cross_attention(x, gx_ref[...], wq_ref, kv_ref, wo_ref)
        hn_ref[...] = _rms(x, gm_ref[...]).astype(BF16)
        o_ref[...] = x

    u = _dot(hn_ref[...], wu_ref[...])
    u = jnp.square(jnp.maximum(u, 0.0)).astype(BF16)
    o_ref[...] += _dot(u, wd_ref[...])

    if final_norm:
        @pl.when(f == pl.num_programs(1) - 1)
        def _():
            o_ref[...] = _rms(o_ref[...], gf_ref[...])


def post_mixer(h, mixes, w_outs, gain_x, wq, kv, wo, gain_mlp, w_up, w_down, final_gain, final_norm, batch):
    rows, d = h.shape
    seq = rows // batch
    mem_len = kv.shape[0] // batch
    ff = w_up.shape[1]
    tm = min(MLP_ROW_TILE, seq)
    tf = min(MLP_FF_TILE, ff)
    tiles_per_batch = seq // tm

    def const(shape):
        return pl.BlockSpec(shape, lambda i, f: (0, 0))

    in_specs = [pl.BlockSpec((tm, m.shape[1]), lambda i, f: (i, 0)) for m in mixes]
    in_specs += [const(w.shape) for w in w_outs]
    in_specs += [pl.BlockSpec((tm, d), lambda i, f: (i, 0)), const((1, d)), const(wq.shape),
                 pl.BlockSpec((mem_len, 2 * XATTN_WIDTH), lambda i, f: (i // tiles_per_batch, 0)),
                 const(wo.shape), const((1, d)),
                 pl.BlockSpec((d, tf), lambda i, f: (0, f)),
                 pl.BlockSpec((tf, d), lambda i, f: (f, 0)),
                 const((1, d))]
    return pl.pallas_call(
        functools.partial(_post_mixer_kernel, n_mix=len(mixes), final_norm=final_norm),
        grid=(rows // tm, ff // tf),
        in_specs=in_specs,
        out_specs=pl.BlockSpec((tm, d), lambda i, f: (i, 0)),
        out_shape=jax.ShapeDtypeStruct((rows, d), F32),
        scratch_shapes=[pltpu.VMEM((tm, d), BF16)],
        compiler_params=_params("parallel", "arbitrary"),
        name="post_mixer",
    )(*mixes, *w_outs, h, gain_x.reshape(1, d), wq, kv, wo, gain_mlp.reshape(1, d), w_up, w_down,
      final_gain.reshape(1, d))


def _fox_gate_kernel(fg_ref, bias_ref, ccol_ref, crow_ref):
    seq = fg_ref.shape[0]
    blk = FOX_TILE
    carry = jnp.zeros((1, LANES), F32)
    for i in range(seq // blk):
        rows = slice(i * blk, (i + 1) * blk)
        c = _cumsum_rows(_log_sigmoid(fg_ref[rows, :] + bias_ref[...])) + carry
        c2 = c * LOG2E
        ccol_ref[rows, :] = c2
        crow_ref[0, :, rows] = c2.T[:FOX_HEADS, :]
        carry = c[blk - 1:blk, :]


def fox_gates(gates, bias, batch, col_block):
    rows = gates.shape[0]
    seq = rows // batch
    return pl.pallas_call(
        _fox_gate_kernel,
        grid=(batch,),
        in_specs=[pl.BlockSpec((seq, LANES), lambda b: (b, col_block)),
                  pl.BlockSpec((1, LANES), lambda b: (0, 0))],
        out_specs=[pl.BlockSpec((seq, LANES), lambda b: (b, 0)),
                   pl.BlockSpec((1, FOX_HEADS, seq), lambda b: (b, 0, 0))],
        out_shape=[jax.ShapeDtypeStruct((rows, LANES), F32),
                   jax.ShapeDtypeStruct((batch, FOX_HEADS, seq), F32)],
        compiler_params=_params("parallel"),
        name="fox_gates",
    )(gates, bias)


FOX_ONES_ROWS = 16


def _fox_kernel(q_ref, k_ref, v_ref, ccol_ref, crow_ref, o_ref, vt_ref, cs_ref, st_ref, p_ref, alpha_ref,
                acc_ref, m_ref):
    pair = pl.program_id(1)
    qi = pl.program_id(2)
    tq, tk = FOX_TQ, FOX_TK
    seq = k_ref.shape[0]
    hd = FOX_HEAD_DIM

    @pl.when(qi == 0)
    def _():
        lane = lax.broadcasted_iota(jnp.int32, (tk, LANES), 1)
        for blk in range(seq // tk):
            rows = slice(blk * tk, (blk + 1) * tk)
            vt = v_ref[rows, :].astype(F32).T
            c = ccol_ref[rows, :]
            for hh in range(2):
                vt_ref[hh, 0:hd, rows] = vt[hh * hd:(hh + 1) * hd, :].astype(BF16)
                vt_ref[hh, hd:hd + FOX_ONES_ROWS, rows] = jnp.ones((FOX_ONES_ROWS, tk), BF16)
                col = jnp.sum(jnp.where(lane == 2 * pair + hh, c, 0.0), axis=-1, keepdims=True)
                cs_ref[hh, rows, :] = jnp.broadcast_to(col, (tk, LANES))

    n_kv = (qi + 1) * (tq // tk)
    q_start = qi * tq
    qt = q_ref[...].astype(F32).T
    row = lax.broadcasted_iota(jnp.int32, (LANES, tq), 0)
    qts = (jnp.where(row < hd, qt, 0.0).astype(BF16), jnp.where(row < hd, 0.0, qt).astype(BF16))
    cts = [crow_ref[0, pl.ds(2 * pair + hh, 1), pl.ds(pl.multiple_of(q_start, tq), tq)]
           for hh in range(2)]
    key_minus_query = (lax.broadcasted_iota(jnp.int32, (tk, tq), 0)
                       - lax.broadcasted_iota(jnp.int32, (tk, tq), 1))

    def scores(j):
        k = k_ref[pl.ds(pl.multiple_of(j * tk, tk), tk), :]
        return [_dot(k, qts[hh]) for hh in range(2)]

    def values_times_probs(j, par):
        start = pl.multiple_of(j * tk, tk)
        return [_dot(vt_ref[hh, :, pl.ds(start, tk)], p_ref[par, hh]) for hh in range(2)]

    m_ref[...] = jnp.full(m_ref.shape, MASK_VALUE, F32)
    acc_ref[...] = jnp.zeros(acc_ref.shape, F32)
    p_ref[1] = jnp.zeros(p_ref.shape[1:], BF16)
    alpha_ref[1] = jnp.ones(alpha_ref.shape[1:], F32)
    first_scores = scores(0)
    for hh in range(2):
        st_ref[0, hh] = first_scores[hh]

    def step(j, masked, has_next):
        par = lax.rem(j, 2)
        prev = 1 - par
        pv = values_times_probs(jnp.maximum(j - 1, 0), prev)
        if has_next:
            next_scores = scores(j + 1)
        start = pl.multiple_of(j * tk, tk)
        for hh in range(2):
            cs = cs_ref[hh, pl.ds(start, tk), :]
            st = st_ref[par, hh] + (cts[hh] - jnp.concatenate([cs] * (tq // LANES), axis=1))
            if masked:
                st = jnp.where(key_minus_query <= q_start - start, st, MASK_VALUE)
            m_old = m_ref[hh]
            m_new = jnp.maximum(m_old, jnp.max(st, axis=0, keepdims=True))
            p_ref[par, hh] = jnp.exp2(st - m_new).astype(BF16)
            acc_ref[hh] = alpha_ref[prev, hh] * acc_ref[hh] + pv[hh]
            alpha_ref[par, hh] = jnp.exp2(m_old - m_new)
            m_ref[hh] = m_new
        if has_next:
            for hh in range(2):
                st_ref[prev, hh] = next_scores[hh]

    def body(j, carry):
        step(j, masked=False, has_next=True)
        return carry

    n_diag = tq // tk
    lax.fori_loop(0, n_kv - n_diag, body, 0)
    for d in range(n_diag):
        step(n_kv - n_diag + d, masked=True, has_next=d < n_diag - 1)
    last = n_kv - 1
    last_par = lax.rem(last, 2)
    pv = values_times_probs(last, last_par)
    outs = []
    for hh in range(2):
        acc = alpha_ref[last_par, hh] * acc_ref[hh] + pv[hh]
        outs.append(acc[0:hd, :] / acc[hd:hd + 1, :])
    o_ref[...] = jnp.concatenate(outs, axis=0).T.astype(o_ref.dtype)


def fox_attention(proj, ccol, crow, batch):
    rows = proj.shape[0]
    seq = rows // batch
    tq, tk = FOX_TQ, FOX_TK
    nq = seq // tq
    acc_rows = FOX_HEAD_DIM + FOX_ONES_ROWS
    return pl.pallas_call(
        _fox_kernel,
        grid=(batch, FOX_PAIRS, nq),
        in_specs=[pl.BlockSpec((tq, LANES), lambda b, p, i: (b * nq + i, p)),
                  pl.BlockSpec((seq, LANES), lambda b, p, i: (b, FOX_PAIRS + p)),
                  pl.BlockSpec((seq, LANES), lambda b, p, i: (b, 2 * FOX_PAIRS + p)),
                  pl.BlockSpec((seq, LANES), lambda b, p, i: (b, 0)),
                  pl.BlockSpec((1, FOX_HEADS, seq), lambda b, p, i: (b, 0, 0))],
        out_specs=pl.BlockSpec((tq, LANES), lambda b, p, i: (b * nq + i, p)),
        out_shape=jax.ShapeDtypeStruct((rows, FOX_WIDTH), BF16),
        scratch_shapes=[pltpu.VMEM((2, acc_rows, seq), BF16), pltpu.VMEM((2, seq, LANES), F32),
                        pltpu.VMEM((2, 2, tk, tq), F32), pltpu.VMEM((2, 2, tk, tq), BF16),
                        pltpu.VMEM((2, 2, 1, tq), F32),
                        pltpu.VMEM((2, acc_rows, tq), F32), pltpu.VMEM((2, 1, tq), F32)],
        compiler_params=_params("parallel", "parallel", "arbitrary"),
        name="fox_attention",
    )(proj, proj, proj, ccol, crow)


def _hgrn_kernel(hq_ref, hi_ref, hg_ref, hf_ref, lbl_ref, gain_ref, o_ref, state_ref, *, layer):
    @pl.when(pl.program_id(2) == 0)
    def _():
        state_ref[...] = jnp.zeros(state_ref.shape, F32)

    logits = lbl_ref[...]
    w = jnp.exp(logits - jnp.max(logits, axis=0, keepdims=True))
    w = w / jnp.sum(w, axis=0, keepdims=True)
    lb = jnp.sum(w[:layer + 1], axis=0, keepdims=True) - w[0:1]
    log_lb = jnp.log(jnp.maximum(lb, LOG_FLOOR))
    log1m_lb = jnp.log1p(-lb)

    ch, sub = HGRN_CHUNK, HGRN_SUB
    n_sub = ch // sub
    lane_c = lax.broadcasted_iota(jnp.int32, (sub, ch), 1)
    row_c = lax.broadcasted_iota(jnp.int32, (sub, ch), 0)

    def chunk(ci):
        rows = slice(ci * ch, (ci + 1) * ch)
        hf = hf_ref[rows, :]
        hq = hq_ref[rows, :].astype(F32)
        v = hi_ref[rows, :].astype(F32)
        hg = hg_ref[rows, :].astype(F32)

        log_gate = log1m_lb + _log_sigmoid(hf)
        log_f = jnp.maximum(log_lb, log_gate) + jnp.log1p(jnp.exp(-jnp.abs(log_lb - log_gate)))
        k = (1.0 - lb) * _sigmoid(-hf)
        q = hq * _sigmoid(hq)
        b = _cumsum_rows(log_f) * LOG2E

        bref = jnp.concatenate(
            [jnp.broadcast_to(b[max(i * sub - 1, 0):max(i * sub - 1, 0) + 1, :], (sub, HGRN_DIM))
             for i in range(n_sub)], axis=0)
        q_rel = (q * jnp.exp2(b - bref)).astype(BF16)

        a_rows = []
        for i in range(n_sub):
            s0 = i * sub
            bq = b[s0:s0 + sub, :]
            qq = q[s0:s0 + sub, :]
            diag = jnp.zeros((sub, ch), F32)
            for s in range(sub):
                bs = b[s0 + s:s0 + s + 1, :]
                ks = k[s0 + s:s0 + s + 1, :]
                e = jnp.exp2(bq - bs)
                col = jnp.sum(qq * ks * e, axis=-1, keepdims=True)
                diag = jnp.where(lane_c == s0 + s, col, diag)
            diag = jnp.where(row_c + s0 >= lane_c, diag, 0.0)
            if i == 0:
                a_rows.append(diag)
            else:
                k_rel = (k * jnp.exp2(jnp.minimum(bref[s0:s0 + 1, :] - b, 0.0))).astype(BF16)
                off = _dot_nt(q_rel[s0:s0 + sub, :], k_rel)
                a_rows.append(jnp.where(lane_c < s0, off, diag))
        a = jnp.concatenate(a_rows, axis=0).astype(BF16)

        state_t = state_ref[...]
        b_last = b[ch - 1:ch, :]
        out = _dot(a, v.astype(BF16)) + _dot_nt((q * jnp.exp2(b)).astype(BF16), state_t.astype(BF16))
        k_end = (k * jnp.exp2(b_last - b)).astype(BF16)
        state_ref[...] = state_t * jnp.exp2(b_last) + _dot(v.T.astype(BF16), k_end)

        y = out * lax.rsqrt(jnp.mean(out * out, axis=-1, keepdims=True) + NORM_EPS) * gain_ref[...]
        o_ref[rows, :] = (y * (hg * _sigmoid(hg))).astype(o_ref.dtype)

    for ci in range(o_ref.shape[0] // ch):
        chunk(ci)


def hgrn2(proj, gates, lb_logits, gain, batch, layer, col_q, col_i, col_g):
    rows = proj.shape[0]
    seq = rows // batch
    tr = min(HGRN_ROWS, seq)
    nt = seq // tr
    n_layers = lb_logits.shape[0]

    def tok(col0):
        return pl.BlockSpec((tr, LANES), lambda b, h, t: (b * nt + t, col0 + h))

    return pl.pallas_call(
        functools.partial(_hgrn_kernel, layer=layer),
        grid=(batch, HGRN_HEADS, nt),
        in_specs=[tok(col_q), tok(col_i), tok(col_g), tok(0),
                  pl.BlockSpec((n_layers, LANES), lambda b, h, t: (0, h)),
                  pl.BlockSpec((1, LANES), lambda b, h, t: (0, h))],
        out_specs=tok(0),
        out_shape=jax.ShapeDtypeStruct((rows, HGRN_WIDTH), BF16),
        scratch_shapes=[pltpu.VMEM((HGRN_DIM, HGRN_DIM), F32)],
        compiler_params=_params("parallel", "parallel", "arbitrary"),
        name="hgrn2",
    )(proj, proj, proj, gates, lb_logits, gain.reshape(1, HGRN_WIDTH))


def _pair_select(first, a, b):
    return jnp.where(first, a, b)


def _ssd_kernel(z_ref, x_ref, bc_ref, dt_ref, cwx_ref, cwbc_ref, cbx_ref, cbbc_ref, dtb_ref, alog_ref,
                dskip_ref, gain_ref, o_ref, state_ref, tailx_ref, tailbc_ref):
    @pl.when(pl.program_id(1) == 0)
    def _():
        state_ref[...] = jnp.zeros(state_ref.shape, F32)
        tailx_ref[...] = jnp.zeros(tailx_ref.shape, F32)
        tailbc_ref[...] = jnp.zeros(tailbc_ref.shape, F32)

    ch = SSM_CHUNK
    a2 = -jnp.exp(alog_ref[...]) * LOG2E
    r = lax.broadcasted_iota(jnp.int32, (ch, ch), 0)
    c = lax.broadcasted_iota(jnp.int32, (ch, ch), 1)
    causal = r >= c
    first = lax.broadcasted_iota(jnp.int32, (ch, LANES), 1) < SSM_HEAD_DIM
    first_row = first[0:1, :]

    def bcast(col):
        return jnp.broadcast_to(col, (ch, LANES))

    def conv(ref, tail_ref, cw_ref, cb_ref, rows, cols):
        raw = ref[rows, cols].astype(F32)
        out = _causal_conv_silu(raw, tail_ref[:, cols], cw_ref[:, cols], cb_ref[:, cols])
        tail_ref[:, cols] = raw[ch - 8:, :]
        return out

    def chunk(rows):
        dt = _softplus(dt_ref[rows, :] + dtb_ref[...])
        cum = _cumsum_rows(dt * a2)
        cum_t = cum.T
        dt_t = dt.T
        exp_cum = jnp.exp2(cum)
        w_end = jnp.exp2(cum[ch - 1:ch, :] - cum) * dt

        for g in range(SSM_GROUPS):
            gx = slice(g * SSM_GROUP_WIDTH, (g + 1) * SSM_GROUP_WIDTH)
            gb = slice(g * SSM_STATE, (g + 1) * SSM_STATE)
            gc = slice(SSM_BC + g * SSM_STATE, SSM_BC + (g + 1) * SSM_STATE)
            xs = conv(x_ref, tailx_ref, cwx_ref, cbx_ref, rows, gx)
            bm = conv(bc_ref, tailbc_ref, cwbc_ref, cbbc_ref, rows, gb)
            cm = conv(bc_ref, tailbc_ref, cwbc_ref, cbbc_ref, rows, gc)
            bm16, cm16 = bm.astype(BF16), cm.astype(BF16)
            cb = _dot_nt(cm16, bm16)
            state = state_ref[g]
            y_state = _dot(cm16, state.astype(BF16))

            y_pairs, wx_pairs, decay_pairs = [], [], []
            for pr in range(2):
                xp = xs[:, pr * LANES:(pr + 1) * LANES]
                xp16 = xp.astype(BF16)
                ys, scale_in, scale_w, ends = [], [], [], []
                for hh in range(2):
                    h = g * 4 + pr * 2 + hh
                    ccol = bcast(cum[:, h:h + 1])
                    seg = jnp.where(causal, ccol - cum_t[h:h + 1, :], MASK_VALUE)
                    m = (cb * jnp.exp2(seg) * dt_t[h:h + 1, :]).astype(BF16)
                    ys.append(_dot(m, xp16))
                    scale_in.append(bcast(exp_cum[:, h:h + 1]))
                    scale_w.append(bcast(w_end[:, h:h + 1]))
                    ends.append(ccol[ch - 1:ch, :])
                y_in = y_state[:, pr * LANES:(pr + 1) * LANES] * _pair_select(first, scale_in[0], scale_in[1])
                y_pairs.append(_pair_select(first, ys[0], ys[1]) + y_in)
                wx_pairs.append((xp * _pair_select(first, scale_w[0], scale_w[1])).astype(BF16))
                decay_pairs.append(jnp.exp2(_pair_select(first_row, ends[0], ends[1])))
            y = jnp.concatenate(y_pairs, axis=-1)
            wx = jnp.concatenate(wx_pairs, axis=-1)
            state_ref[g] = state * jnp.concatenate(decay_pairs, axis=-1) + _dot(bm.T.astype(BF16), wx)

            y = y + dskip_ref[:, gx] * xs
            zg = z_ref[rows, gx].astype(F32)
            y = y * (zg * _sigmoid(zg))
            y = y * lax.rsqrt(jnp.mean(y * y, axis=-1, keepdims=True) + NORM_EPS) * gain_ref[:, gx]
            o_ref[rows, gx] = y.astype(o_ref.dtype)

    for ci in range(o_ref.shape[0] // ch):
        chunk(slice(ci * ch, (ci + 1) * ch))


def ssd_mixer(proj, dt_raw, conv_w, conv_b, dt_bias, a_log, d_skip, gain, batch):
    rows = proj.shape[0]
    seq = rows // batch
    tr = min(SSM_ROWS, seq)
    nt = seq // tr

    def tok(width, col):
        return pl.BlockSpec((tr, width), lambda b, c: (b * nt + c, col))

    def par(nrows, width, col):
        return pl.BlockSpec((nrows, width), lambda b, c: (0, col))

    return pl.pallas_call(
        _ssd_kernel,
        grid=(batch, nt),
        in_specs=[tok(SSM_INNER, 0), tok(SSM_INNER, 1), tok(2 * SSM_BC, 2), tok(LANES, 0),
                  par(SSM_CONV, SSM_INNER, 0), par(SSM_CONV, 2 * SSM_BC, 1),
                  par(1, SSM_INNER, 0), par(1, 2 * SSM_BC, 1),
                  par(1, LANES, 0), par(1, LANES, 0), par(1, SSM_INNER, 0), par(1, SSM_INNER, 0)],
        out_specs=tok(SSM_INNER, 0),
        out_shape=jax.ShapeDtypeStruct((rows, SSM_INNER), BF16),
        scratch_shapes=[pltpu.VMEM((SSM_GROUPS, SSM_STATE, SSM_GROUP_WIDTH), F32),
                        pltpu.VMEM((8, SSM_INNER), F32), pltpu.VMEM((8, 2 * SSM_BC), F32)],
        compiler_params=_params("parallel", "arbitrary"),
        name="ssd_mixer",
    )(proj, proj, proj, dt_raw, conv_w, conv_w, conv_b, conv_b, dt_bias, a_log, d_skip, gain)


def _pad_lanes(x):
    return jnp.pad(x, ((0, 0), (0, LANES - x.shape[1])))


def _split_cols(w, sizes):
    out, lo = [], 0
    for s in sizes:
        out.append(w[:, lo:lo + s])
        lo += s
    return out


def kernel(x, mem, mem_norm, norm_mix, norm_xattn, norm_mlp, norm_final, ev_in_proj, fox_fgate_bias,
           hgrn_lb_logits, hgrn_out_norm, ev_out_proj, ssm_in_proj, ssm_conv_w, ssm_conv_b, ssm_dt_bias,
           ssm_A_log, ssm_D, ssm_norm, ssm_out_proj, xa_q, xa_kv, xa_o, mlp_up, mlp_down):
    batch, seq, d = x.shape
    depth = norm_mix.shape[0]
    h = x.reshape(batch * seq, d)
    mem2 = mem.reshape(-1, d)

    for layer in range(depth):
        if layer % 2 == 0:
            e = layer // 2
            fq, fk, fv, fg, hq, hf, hi, hg = _split_cols(
                ev_in_proj[e], (FOX_WIDTH, FOX_WIDTH, FOX_WIDTH, FOX_HEADS,
                                HGRN_WIDTH, HGRN_WIDTH, HGRN_WIDTH, HGRN_WIDTH))
            w_main = jnp.concatenate([fq * (FOX_HEAD_DIM ** -0.5 * LOG2E), fk, fv, hq, hi, hg], axis=1).astype(BF16)
            w_gate = jnp.concatenate([hf, _pad_lanes(fg)], axis=1).astype(BF16)
            proj, gates = norm_proj(h, norm_mix[layer], w_main, w_gate, EVEN_PROJ_TILE)
            ccol, crow = fox_gates(gates, _pad_lanes(fox_fgate_bias[e].reshape(1, -1)), batch,
                                   HGRN_WIDTH // LANES)
            a_out = fox_attention(proj, ccol, crow, batch)
            blocks = FOX_WIDTH // LANES
            b_out = hgrn2(proj, gates, hgrn_lb_logits, hgrn_out_norm[e], batch, e,
                          3 * blocks, 3 * blocks + HGRN_HEADS, 3 * blocks + 2 * HGRN_HEADS)
            w_out = ev_out_proj[e].astype(BF16)
            mixes, w_outs = [a_out, b_out], [w_out[:FOX_WIDTH], w_out[FOX_WIDTH:]]
        else:
            o = layer // 2
            w_in = ssm_in_proj[o]
            n_main = 2 * SSM_INNER + 2 * SSM_BC
            proj, dt_raw = norm_proj(h, norm_mix[layer], w_in[:, :n_main].astype(BF16),
                                     _pad_lanes(w_in[:, n_main:]).astype(BF16), SSM_PROJ_TILE)
            y = ssd_mixer(proj, dt_raw, ssm_conv_w[o], ssm_conv_b[o].reshape(1, -1),
                          _pad_lanes(ssm_dt_bias[o].reshape(1, -1)), _pad_lanes(ssm_A_log[o].reshape(1, -1)),
                          jnp.repeat(ssm_D[o], SSM_HEAD_DIM).reshape(1, -1), ssm_norm[o].reshape(1, -1), batch)
            mixes, w_outs = [y], [ssm_out_proj[o].astype(BF16)]

        kv = norm_matmul(mem2, mem_norm, xa_kv[layer].astype(BF16), BF16, 2 * XATTN_WIDTH)
        h = post_mixer(h, mixes, w_outs, norm_xattn[layer], xa_q[layer].astype(BF16), kv,
                       xa_o[layer].astype(BF16), norm_mlp[layer], mlp_up[layer].astype(BF16),
                       mlp_down[layer].astype(BF16), norm_final, layer == depth - 1, batch)

    return h.reshape(batch, seq, d)
```

```python
import functools

import jax
import jax.numpy as jnp
from jax import lax
from jax.experimental import pallas as pl
from jax.experimental.pallas import tpu as pltpu

F32 = jnp.float32
BF16 = jnp.bfloat16
HIGHEST = lax.Precision.HIGHEST

NORM_EPS = 1e-6
MASK_VALUE = -1e30
LOG_FLOOR = 1e-30
LOG2E = 1.4426950408889634

LANES = 128
VMEM_LIMIT_BYTES = 56 * 1024 * 1024

FOX_HEADS = 8
FOX_HEAD_DIM = 64
FOX_WIDTH = FOX_HEADS * FOX_HEAD_DIM
FOX_PAIRS = FOX_WIDTH // LANES
FOX_TILE = 256
FOX_TQ = 512
FOX_TK = 512

HGRN_HEADS = 4
HGRN_DIM = 128
HGRN_WIDTH = HGRN_HEADS * HGRN_DIM
HGRN_CHUNK = 64
HGRN_SUB = 16
HGRN_ROWS = 1024

SSM_INNER = 2048
SSM_HEAD_DIM = 64
SSM_HEADS = SSM_INNER // SSM_HEAD_DIM
SSM_GROUPS = 8
SSM_GROUP_WIDTH = SSM_INNER // SSM_GROUPS
SSM_STATE = 128
SSM_CONV = 4
SSM_CHUNK = 128
SSM_ROWS = 256
SSM_BC = SSM_GROUPS * SSM_STATE

XATTN_HEADS = 4
XATTN_HEAD_DIM = 128
XATTN_WIDTH = XATTN_HEADS * XATTN_HEAD_DIM

ROW_TILE = 512
MLP_ROW_TILE = 1024
MLP_FF_TILE = 1024
EVEN_PROJ_TILE = 1536
SSM_PROJ_TILE = 2048


def _params(*semantics):
    return pltpu.CompilerParams(dimension_semantics=semantics, vmem_limit_bytes=VMEM_LIMIT_BYTES)


def _rms(x, gain):
    ms = jnp.mean(x * x, axis=-1, keepdims=True)
    return x * lax.rsqrt(ms + NORM_EPS) * gain


def _log_sigmoid(x):
    return jnp.minimum(x, 0.0) - jnp.log1p(jnp.exp(-jnp.abs(x)))


def _sigmoid(x):
    return 1.0 / (1.0 + jnp.exp(-x))


def _softplus(x):
    return jnp.maximum(x, 0.0) + jnp.log1p(jnp.exp(-jnp.abs(x)))


def _dot(a, b):
    return jnp.dot(a, b, preferred_element_type=F32)


def _dot_nt(a, b):
    return lax.dot_general(a, b, (((1,), (1,)), ((), ())), preferred_element_type=F32)


def _cumsum_rows(x):
    n = x.shape[0]
    r = lax.broadcasted_iota(jnp.int32, (n, n), 0)
    c = lax.broadcasted_iota(jnp.int32, (n, n), 1)
    tri = jnp.where(r >= c, 1.0, 0.0).astype(F32)
    return jnp.dot(tri, x, precision=HIGHEST, preferred_element_type=F32)


def _norm_matmul_kernel(h_ref, g_ref, w_ref, o_ref, hn_ref):
    @pl.when(pl.program_id(1) == 0)
    def _():
        hn_ref[...] = _rms(h_ref[...], g_ref[...]).astype(BF16)

    o_ref[...] = _dot(hn_ref[...], w_ref[...]).astype(o_ref.dtype)


def norm_matmul(h, gain, w, out_dtype, tn):
    rows, d = h.shape
    n = w.shape[1]
    tm = min(ROW_TILE * 2, rows)
    return pl.pallas_call(
        _norm_matmul_kernel,
        grid=(rows // tm, n // tn),
        in_specs=[pl.BlockSpec((tm, d), lambda i, j: (i, 0)),
                  pl.BlockSpec((1, d), lambda i, j: (0, 0)),
                  pl.BlockSpec((d, tn), lambda i, j: (0, j))],
        out_specs=pl.BlockSpec((tm, tn), lambda i, j: (i, j)),
        out_shape=jax.ShapeDtypeStruct((rows, n), out_dtype),
        scratch_shapes=[pltpu.VMEM((tm, d), BF16)],
        compiler_params=_params("parallel", "arbitrary"),
        name="norm_matmul",
    )(h, gain.reshape(1, d), w)


def _norm_proj_kernel(h_ref, g_ref, w_ref, wg_ref, o_ref, og_ref, hn_ref):
    @pl.when(pl.program_id(1) == 0)
    def _():
        hn = _rms(h_ref[...], g_ref[...]).astype(BF16)
        hn_ref[...] = hn
        og_ref[...] = _dot(hn, wg_ref[...])

    o_ref[...] = _dot(hn_ref[...], w_ref[...]).astype(o_ref.dtype)


def norm_proj(h, gain, w, w_gate, tn):
    rows, d = h.shape
    n, ng = w.shape[1], w_gate.shape[1]
    tm = min(ROW_TILE * 2, rows)
    return pl.pallas_call(
        _norm_proj_kernel,
        grid=(rows // tm, n // tn),
        in_specs=[pl.BlockSpec((tm, d), lambda i, j: (i, 0)),
                  pl.BlockSpec((1, d), lambda i, j: (0, 0)),
                  pl.BlockSpec((d, tn), lambda i, j: (0, j)),
                  pl.BlockSpec((d, ng), lambda i, j: (0, 0))],
        out_specs=[pl.BlockSpec((tm, tn), lambda i, j: (i, j)),
                   pl.BlockSpec((tm, ng), lambda i, j: (i, 0))],
        out_shape=[jax.ShapeDtypeStruct((rows, n), BF16), jax.ShapeDtypeStruct((rows, ng), F32)],
        scratch_shapes=[pltpu.VMEM((tm, d), BF16)],
        compiler_params=_params("parallel", "arbitrary"),
        name="norm_proj",
    )(h, gain.reshape(1, d), w, w_gate)


def _causal_conv_silu(x, tail, w, bias):
    row8 = lax.broadcasted_iota(jnp.int32, tail.shape, 0)
    acc = x * w[SSM_CONV - 1:SSM_CONV, :] + bias
    for j in range(1, SSM_CONV):
        rolled = pltpu.roll(x, j, axis=0)
        head = jnp.where(row8 < j, pltpu.roll(tail, j, axis=0), rolled[0:8, :])
        shifted = jnp.concatenate([head, rolled[8:, :]], axis=0)
        acc = acc + shifted * w[SSM_CONV - 1 - j:SSM_CONV - j, :]
    return acc * _sigmoid(acc)


def _cross_attention(x, gain, wq_ref, kv_ref, wo_ref):
    hn = _rms(x, gain).astype(BF16)
    q = _dot(hn, wq_ref[...]).astype(BF16)
    scale = XATTN_HEAD_DIM ** -0.5
    outs = []
    for hd in range(XATTN_HEADS):
        lo = hd * XATTN_HEAD_DIM
        qh = q[:, lo:lo + XATTN_HEAD_DIM]
        kh = kv_ref[:, lo:lo + XATTN_HEAD_DIM]
        vh = kv_ref[:, XATTN_WIDTH + lo:XATTN_WIDTH + lo + XATTN_HEAD_DIM]
        s = _dot_nt(qh, kh) * scale
        p = jnp.exp(s - jnp.max(s, axis=-1, keepdims=True))
        denom = jnp.sum(p, axis=-1, keepdims=True)
        outs.append((_dot(p.astype(BF16), vh) / denom).astype(BF16))
    return _dot(jnp.concatenate(outs, axis=-1), wo_ref[...])


def _post_mixer_kernel(*refs, n_mix, final_norm):
    mix_refs, wout_refs = refs[:n_mix], refs[n_mix:2 * n_mix]
    (h_ref, gx_ref, wq_ref, kv_ref, wo_ref, gm_ref, wu_ref, wd_ref, gf_ref,
     o_ref, hn_ref) = refs[2 * n_mix:]
    f = pl.program_id(1)

    @pl.when(f == 0)
    def _():
        x = h_ref[...]
        for m_ref, w_ref in zip(mix_refs, wout_refs):
            x = x + _dot(m_ref[...], w_ref[...])
        x = x + _cross_attention(x, gx_ref[...], wq_ref, kv_ref, wo_ref)
        hn_ref[...] = _rms(x, gm_ref[...]).astype(BF16)
        o_ref[...] = x

    u = _dot(hn_ref[...], wu_ref[...])
    u = jnp.square(jnp.maximum(u, 0.0)).astype(BF16)
    o_ref[...] += _dot(u, wd_ref[...])

    if final_norm:
        @pl.when(f == pl.num_programs(1) - 1)
        def _():
            o_ref[...] = _rms(o_ref[...], gf_ref[...])


def post_mixer(h, mixes, w_outs, gain_x, wq, kv, wo, gain_mlp, w_up, w_down, final_gain, final_norm, batch):
    rows, d = h.shape
    seq = rows // batch
    mem_len = kv.shape[0] // batch
    ff = w_up.shape[1]
    tm = min(MLP_ROW_TILE, seq)
    tf = min(MLP_FF_TILE, ff)
    tiles_per_batch = seq // tm

    def const(shape):
        return pl.BlockSpec(shape, lambda i, f: (0, 0))

    in_specs = [pl.BlockSpec((tm, m.shape[1]), lambda i, f: (i, 0)) for m in mixes]
    in_specs += [const(w.shape) for w in w_outs]
    in_specs += [pl.BlockSpec((tm, d), lambda i, f: (i, 0)), const((1, d)), const(wq.shape),
                 pl.BlockSpec((mem_len, 2 * XATTN_WIDTH), lambda i, f: (i // tiles_per_batch, 0)),
                 const(wo.shape), const((1, d)),
                 pl.BlockSpec((d, tf), lambda i, f: (0, f)),
                 pl.BlockSpec((tf, d), lambda i, f: (f, 0)),
                 const((1, d))]
    return pl.pallas_call(
        functools.partial(_post_mixer_kernel, n_mix=len(mixes), final_norm=final_norm),
        grid=(rows // tm, ff // tf),
        in_specs=in_specs,
        out_specs=pl.BlockSpec((tm, d), lambda i, f: (i, 0)),
        out_shape=jax.ShapeDtypeStruct((rows, d), F32),
        scratch_shapes=[pltpu.VMEM((tm, d), BF16)],
        compiler_params=_params("parallel", "arbitrary"),
        name="post_mixer",
    )(*mixes, *w_outs, h, gain_x.reshape(1, d), wq, kv, wo, gain_mlp.reshape(1, d), w_up, w_down,
      final_gain.reshape(1, d))


def _fox_gate_kernel(fg_ref, bias_ref, ccol_ref, crow_ref):
    seq = fg_ref.shape[0]
    blk = FOX_TILE
    carry = jnp.zeros((1, LANES), F32)
    for i in range(seq // blk):
        rows = slice(i * blk, (i + 1) * blk)
        c = _cumsum_rows(_log_sigmoid(fg_ref[rows, :] + bias_ref[...])) + carry
        c2 = c * LOG2E
        ccol_ref[rows, :] = c2
        crow_ref[0, :, rows] = c2.T[:FOX_HEADS, :]
        carry = c[blk - 1:blk, :]


def fox_gates(gates, bias, batch, col_block):
    rows = gates.shape[0]
    seq = rows // batch
    return pl.pallas_call(
        _fox_gate_kernel,
        grid=(batch,),
        in_specs=[pl.BlockSpec((seq, LANES), lambda b: (b, col_block)),
                  pl.BlockSpec((1, LANES), lambda b: (0, 0))],
        out_specs=[pl.BlockSpec((seq, LANES), lambda b: (b, 0)),
                   pl.BlockSpec((1, FOX_HEADS, seq), lambda b: (b, 0, 0))],
        out_shape=[jax.ShapeDtypeStruct((rows, LANES), F32),
                   jax.ShapeDtypeStruct((batch, FOX_HEADS, seq), F32)],
        compiler_params=_params("parallel"),
        name="fox_gates",
    )(gates, bias)


FOX_ONES_ROWS = 16


def _fox_kernel(q_ref, k_ref, v_ref, ccol_ref, crow_ref, o_ref, vt_ref, cs_ref, st_ref, p_ref, alpha_ref,
                acc_ref, m_ref):
    pair = pl.program_id(1)
    tq, tk = FOX_TQ, FOX_TK
    seq = k_ref.shape[0]
    hd = FOX_HEAD_DIM

    lane = lax.broadcasted_iota(jnp.int32, (tk, LANES), 1)
    for blk in range(seq // tk):
        rows = slice(blk * tk, (blk + 1) * tk)
        vt = v_ref[rows, :].astype(F32).T
        c = ccol_ref[rows, :]
        for hh in range(2):
            vt_ref[hh, 0:hd, rows] = vt[hh * hd:(hh + 1) * hd, :].astype(BF16)
            vt_ref[hh, hd:hd + FOX_ONES_ROWS, rows] = jnp.ones((FOX_ONES_ROWS, tk), BF16)
            col = jnp.sum(jnp.where(lane == 2 * pair + hh, c, 0.0), axis=-1, keepdims=True)
            cs_ref[hh, rows, :] = jnp.broadcast_to(col, (tk, LANES))

    row = lax.broadcasted_iota(jnp.int32, (LANES, tq), 0)
    key_minus_query = (lax.broadcasted_iota(jnp.int32, (tk, tq), 0)
                       - lax.broadcasted_iota(jnp.int32, (tk, tq), 1))

    def query_tile(qi):
        slot = qi % 2
        q_start = qi * tq
        n_kv = -(-(q_start + tq) // tk)
        qt = q_ref[q_start:q_start + tq, :].astype(F32).T
        qts = (jnp.where(row < hd, qt, 0.0).astype(BF16), jnp.where(row < hd, 0.0, qt).astype(BF16))
        cts = [crow_ref[0, pl.ds(2 * pair + hh, 1), q_start:q_start + tq] for hh in range(2)]

        def scores(j):
            k = k_ref[j * tk:(j + 1) * tk, :]
            return [_dot(k, qts[hh]) for hh in range(2)]

        def values_times_probs(j):
            return [_dot(vt_ref[hh, :, j * tk:(j + 1) * tk], p_ref[slot, j % 2, hh]) for hh in range(2)]

        first_scores = scores(0)
        for hh in range(2):
            st_ref[slot, 0, hh] = first_scores[hh]
            m_ref[slot, hh] = jnp.full(m_ref.shape[2:], MASK_VALUE, F32)

        for j in range(n_kv):
            par, prev = j % 2, 1 - j % 2
            if j > 0:
                pv = values_times_probs(j - 1)
            if j + 1 < n_kv:
                next_scores = scores(j + 1)
            for hh in range(2):
                cs = cs_ref[hh, j * tk:(j + 1) * tk, :]
                st = st_ref[slot, par, hh] + (cts[hh] - jnp.concatenate([cs] * (tq // LANES), axis=1))
                if (j + 1) * tk - 1 > q_start:
                    st = jnp.where(key_minus_query <= q_start - j * tk, st, MASK_VALUE)
                m_old = m_ref[slot, hh]
                m_new = jnp.maximum(m_old, jnp.max(st, axis=0, keepdims=True))
                p_ref[slot, par, hh] = jnp.exp2(st - m_new).astype(BF16)
                if j == 1:
                    acc_ref[slot, hh] = pv[hh]
                elif j > 1:
                    acc_ref[slot, hh] = alpha_ref[slot, prev, hh] * acc_ref[slot, hh] + pv[hh]
                alpha_ref[slot, par, hh] = jnp.exp2(m_old - m_new)
                m_ref[slot, hh] = m_new
            if j + 1 < n_kv:
                for hh in range(2):
                    st_ref[slot, prev, hh] = next_scores[hh]

        last = n_kv - 1
        pv = values_times_probs(last)
        outs = []
        for hh in range(2):
            acc = pv[hh] if last == 0 else alpha_ref[slot, last % 2, hh] * acc_ref[slot, hh] + pv[hh]
            outs.append(acc[0:hd, :] / acc[hd:hd + 1, :])
        o_ref[q_start:q_start + tq, :] = jnp.concatenate(outs, axis=0).T.astype(o_ref.dtype)

    for qi in range(seq // tq):
        query_tile(qi)


def fox_attention(proj, ccol, crow, batch):
    rows = proj.shape[0]
    seq = rows // batch
    tq, tk = FOX_TQ, FOX_TK
    acc_rows = FOX_HEAD_DIM + FOX_ONES_ROWS
    return pl.pallas_call(
        _fox_kernel,
        grid=(batch, FOX_PAIRS),
        in_specs=[pl.BlockSpec((seq, LANES), lambda b, p: (b, p)),
                  pl.BlockSpec((seq, LANES), lambda b, p: (b, FOX_PAIRS + p)),
                  pl.BlockSpec((seq, LANES), lambda b, p: (b, 2 * FOX_PAIRS + p)),
                  pl.BlockSpec((seq, LANES), lambda b, p: (b, 0)),
                  pl.BlockSpec((1, FOX_HEADS, seq), lambda b, p: (b, 0, 0))],
        out_specs=pl.BlockSpec((seq, LANES), lambda b, p: (b, p)),
        out_shape=jax.ShapeDtypeStruct((rows, FOX_WIDTH), BF16),
        scratch_shapes=[pltpu.VMEM((2, acc_rows, seq), BF16), pltpu.VMEM((2, seq, LANES), F32),
                        pltpu.VMEM((2, 2, 2, tk, tq), F32), pltpu.VMEM((2, 2, 2, tk, tq), BF16),
                        pltpu.VMEM((2, 2, 2, 1, tq), F32),
                        pltpu.VMEM((2, 2, acc_rows, tq), F32), pltpu.VMEM((2, 2, 1, tq), F32)],
        compiler_params=_params("parallel", "parallel"),
        name="fox_attention",
    )(proj, proj, proj, ccol, crow)


def _hgrn_kernel(hq_ref, hi_ref, hg_ref, hf_ref, lbl_ref, gain_ref, o_ref, state_ref, *, layer):
    @pl.when(pl.program_id(2) == 0)
    def _():
        state_ref[...] = jnp.zeros(state_ref.shape, F32)

    logits = lbl_ref[...]
    w = jnp.exp(logits - jnp.max(logits, axis=0, keepdims=True))
    w = w / jnp.sum(w, axis=0, keepdims=True)
    lb = jnp.sum(w[:layer + 1], axis=0, keepdims=True) - w[0:1]
    log_lb = jnp.log(jnp.maximum(lb, LOG_FLOOR))
    log1m_lb = jnp.log1p(-lb)

    ch, sub = HGRN_CHUNK, HGRN_SUB
    n_sub = ch // sub
    lane_c = lax.broadcasted_iota(jnp.int32, (sub, ch), 1)
    row_c = lax.broadcasted_iota(jnp.int32, (sub, ch), 0)

    def chunk(ci):
        rows = slice(ci * ch, (ci + 1) * ch)
        hf = hf_ref[rows, :]
        hq = hq_ref[rows, :].astype(F32)
        v = hi_ref[rows, :].astype(F32)
        hg = hg_ref[rows, :].astype(F32)

        log_gate = log1m_lb + _log_sigmoid(hf)
        log_f = jnp.maximum(log_lb, log_gate) + jnp.log1p(jnp.exp(-jnp.abs(log_lb - log_gate)))
        k = (1.0 - lb) * _sigmoid(-hf)
        q = hq * _sigmoid(hq)
        b = _cumsum_rows(log_f) * LOG2E

        bref = jnp.concatenate(
            [jnp.broadcast_to(b[max(i * sub - 1, 0):max(i * sub - 1, 0) + 1, :], (sub, HGRN_DIM))
             for i in range(n_sub)], axis=0)
        q_rel = (q * jnp.exp2(b - bref)).astype(BF16)

        a_rows = []
        for i in range(n_sub):
            s0 = i * sub
            bq = b[s0:s0 + sub, :]
            qq = q[s0:s0 + sub, :]
            diag = jnp.zeros((sub, ch), F32)
            for s in range(sub):
                bs = b[s0 + s:s0 + s + 1, :]
                ks = k[s0 + s:s0 + s + 1, :]
                e = jnp.exp2(bq - bs)
                col = jnp.sum(qq * ks * e, axis=-1, keepdims=True)
                diag = jnp.where(lane_c == s0 + s, col, diag)
            diag = jnp.where(row_c + s0 >= lane_c, diag, 0.0)
            if i == 0:
                a_rows.append(diag)
            else:
                k_rel = (k * jnp.exp2(jnp.minimum(bref[s0:s0 + 1, :] - b, 0.0))).astype(BF16)
                off = _dot_nt(q_rel[s0:s0 + sub, :], k_rel)
                a_rows.append(jnp.where(lane_c < s0, off, diag))
        a = jnp.concatenate(a_rows, axis=0).astype(BF16)

        state_t = state_ref[...]
        b_last = b[ch - 1:ch, :]
        out = _dot(a, v.astype(BF16)) + _dot_nt((q * jnp.exp2(b)).astype(BF16), state_t.astype(BF16))
        k_end = (k * jnp.exp2(b_last - b)).astype(BF16)
        state_ref[...] = state_t * jnp.exp2(b_last) + _dot(v.T.astype(BF16), k_end)

        y = out * lax.rsqrt(jnp.mean(out * out, axis=-1, keepdims=True) + NORM_EPS) * gain_ref[...]
        o_ref[rows, :] = (y * (hg * _sigmoid(hg))).astype(o_ref.dtype)

    for ci in range(o_ref.shape[0] // ch):
        chunk(ci)


def hgrn2(proj, gates, lb_logits, gain, batch, layer, col_q, col_i, col_g):
    rows = proj.shape[0]
    seq = rows // batch
    tr = min(HGRN_ROWS, seq)
    nt = seq // tr
    n_layers = lb_logits.shape[0]

    def tok(col0):
        return pl.BlockSpec((tr, LANES), lambda b, h, t: (b * nt + t, col0 + h))

    return pl.pallas_call(
        functools.partial(_hgrn_kernel, layer=layer),
        grid=(batch, HGRN_HEADS, nt),
        in_specs=[tok(col_q), tok(col_i), tok(col_g), tok(0),
                  pl.BlockSpec((n_layers, LANES), lambda b, h, t: (0, h)),
                  pl.BlockSpec((1, LANES), lambda b, h, t: (0, h))],
        out_specs=tok(0),
        out_shape=jax.ShapeDtypeStruct((rows, HGRN_WIDTH), BF16),
        scratch_shapes=[pltpu.VMEM((HGRN_DIM, HGRN_DIM), F32)],
        compiler_params=_params("parallel", "parallel", "arbitrary"),
        name="hgrn2",
    )(proj, proj, proj, gates, lb_logits, gain.reshape(1, HGRN_WIDTH))


def _pair_select(first, a, b):
    return jnp.where(first, a, b)


def _ssd_kernel(z_ref, x_ref, bc_ref, dt_ref, cwx_ref, cwbc_ref, cbx_ref, cbbc_ref, dtb_ref, alog_ref,
                dskip_ref, gain_ref, o_ref, state_ref, tailx_ref, tailbc_ref):
    @pl.when(pl.program_id(1) == 0)
    def _():
        state_ref[...] = jnp.zeros(state_ref.shape, F32)
        tailx_ref[...] = jnp.zeros(tailx_ref.shape, F32)
        tailbc_ref[...] = jnp.zeros(tailbc_ref.shape, F32)

    ch = SSM_CHUNK
    a2 = -jnp.exp(alog_ref[...]) * LOG2E
    r = lax.broadcasted_iota(jnp.int32, (ch, ch), 0)
    c = lax.broadcasted_iota(jnp.int32, (ch, ch), 1)
    causal = r >= c
    first = lax.broadcasted_iota(jnp.int32, (ch, LANES), 1) < SSM_HEAD_DIM
    first_row = first[0:1, :]

    def bcast(col):
        return jnp.broadcast_to(col, (ch, LANES))

    def conv(ref, tail_ref, cw_ref, cb_ref, rows, cols):
        raw = ref[rows, cols].astype(F32)
        out = _causal_conv_silu(raw, tail_ref[:, cols], cw_ref[:, cols], cb_ref[:, cols])
        tail_ref[:, cols] = raw[ch - 8:, :]
        return out

    def chunk(rows):
        dt = _softplus(dt_ref[rows, :] + dtb_ref[...])
        cum = _cumsum_rows(dt * a2)
        cum_t = cum.T
        dt_t = dt.T
        exp_cum = jnp.exp2(cum)
        w_end = jnp.exp2(cum[ch - 1:ch, :] - cum) * dt

        for g in range(SSM_GROUPS):
            gx = slice(g * SSM_GROUP_WIDTH, (g + 1) * SSM_GROUP_WIDTH)
            gb = slice(g * SSM_STATE, (g + 1) * SSM_STATE)
            gc = slice(SSM_BC + g * SSM_STATE, SSM_BC + (g + 1) * SSM_STATE)
            xs = conv(x_ref, tailx_ref, cwx_ref, cbx_ref, rows, gx)
            bm = conv(bc_ref, tailbc_ref, cwbc_ref, cbbc_ref, rows, gb)
            cm = conv(bc_ref, tailbc_ref, cwbc_ref, cbbc_ref, rows, gc)
            bm16, cm16 = bm.astype(BF16), cm.astype(BF16)
            cb = _dot_nt(cm16, bm16)
            state = state_ref[g]
            y_state = _dot(cm16, state.astype(BF16))

            y_pairs, wx_pairs, decay_pairs = [], [], []
            for pr in range(2):
                xp = xs[:, pr * LANES:(pr + 1) * LANES]
                xp16 = xp.astype(BF16)
                ys, scale_in, scale_w, ends = [], [], [], []
                for hh in range(2):
                    h = g * 4 + pr * 2 + hh
                    ccol = bcast(cum[:, h:h + 1])
                    seg = jnp.where(causal, ccol - cum_t[h:h + 1, :], MASK_VALUE)
                    m = (cb * jnp.exp2(seg) * dt_t[h:h + 1, :]).astype(BF16)
                    ys.append(_dot(m, xp16))
                    scale_in.append(bcast(exp_cum[:, h:h + 1]))
                    scale_w.append(bcast(w_end[:, h:h + 1]))
                    ends.append(ccol[ch - 1:ch, :])
                y_in = y_state[:, pr * LANES:(pr + 1) * LANES] * _pair_select(first, scale_in[0], scale_in[1])
                y_pairs.append(_pair_select(first, ys[0], ys[1]) + y_in)
                wx_pairs.append((xp * _pair_select(first, scale_w[0], scale_w[1])).astype(BF16))
                decay_pairs.append(jnp.exp2(_pair_select(first_row, ends[0], ends[1])))
            y = jnp.concatenate(y_pairs, axis=-1)
            wx = jnp.concatenate(wx_pairs, axis=-1)
            state_ref[g] = state * jnp.concatenate(decay_pairs, axis=-1) + _dot(bm.T.astype(BF16), wx)

            y = y + dskip_ref[:, gx] * xs
            zg = z_ref[rows, gx].astype(F32)
            y = y * (zg * _sigmoid(zg))
            y = y * lax.rsqrt(jnp.mean(y * y, axis=-1, keepdims=True) + NORM_EPS) * gain_ref[:, gx]
            o_ref[rows, gx] = y.astype(o_ref.dtype)

    for ci in range(o_ref.shape[0] // ch):
        chunk(slice(ci * ch, (ci + 1) * ch))


def ssd_mixer(proj, dt_raw, conv_w, conv_b, dt_bias, a_log, d_skip, gain, batch):
    rows = proj.shape[0]
    seq = rows // batch
    tr = min(SSM_ROWS, seq)
    nt = seq // tr

    def tok(width, col):
        return pl.BlockSpec((tr, width), lambda b, c: (b * nt + c, col))

    def par(nrows, width, col):
        return pl.BlockSpec((nrows, width), lambda b, c: (0, col))

    return pl.pallas_call(
        _ssd_kernel,
        grid=(batch, nt),
        in_specs=[tok(SSM_INNER, 0), tok(SSM_INNER, 1), tok(2 * SSM_BC, 2), tok(LANES, 0),
                  par(SSM_CONV, SSM_INNER, 0), par(SSM_CONV, 2 * SSM_BC, 1),
                  par(1, SSM_INNER, 0), par(1, 2 * SSM_BC, 1),
                  par(1, LANES, 0), par(1, LANES, 0), par(1, SSM_INNER, 0), par(1, SSM_INNER, 0)],
        out_specs=tok(SSM_INNER, 0),
        out_shape=jax.ShapeDtypeStruct((rows, SSM_INNER), BF16),
        scratch_shapes=[pltpu.VMEM((SSM_GROUPS, SSM_STATE, SSM_GROUP_WIDTH), F32),
                        pltpu.VMEM((8, SSM_INNER), F32), pltpu.VMEM((8, 2 * SSM_BC), F32)],
        compiler_params=_params("parallel", "arbitrary"),
        name="ssd_mixer",
    )(proj, proj, proj, dt_raw, conv_w, conv_w, conv_b, conv_b, dt_bias, a_log, d_skip, gain)


def _pad_lanes(x):
    return jnp.pad(x, ((0, 0), (0, LANES - x.shape[1])))


def _split_cols(w, sizes):
    out, lo = [], 0
    for s in sizes:
        out.append(w[:, lo:lo + s])
        lo += s
    return out


def kernel(x, mem, mem_norm, norm_mix, norm_xattn, norm_mlp, norm_final, ev_in_proj, fox_fgate_bias,
           hgrn_lb_logits, hgrn_out_norm, ev_out_proj, ssm_in_proj, ssm_conv_w, ssm_conv_b, ssm_dt_bias,
           ssm_A_log, ssm_D, ssm_norm, ssm_out_proj, xa_q, xa_kv, xa_o, mlp_up, mlp_down):
    batch, seq, d = x.shape
    depth = norm_mix.shape[0]
    h = x.reshape(batch * seq, d)
    mem2 = mem.reshape(-1, d)

    for layer in range(depth):
        if layer % 2 == 0:
            e = layer // 2
            fq, fk, fv, fg, hq, hf, hi, hg = _split_cols(
                ev_in_proj[e], (FOX_WIDTH, FOX_WIDTH, FOX_WIDTH, FOX_HEADS,
                                HGRN_WIDTH, HGRN_WIDTH, HGRN_WIDTH, HGRN_WIDTH))
            w_main = jnp.concatenate([fq * (FOX_HEAD_DIM ** -0.5 * LOG2E), fk, fv, hq, hi, hg], axis=1).astype(BF16)
            w_gate = jnp.concatenate([hf, _pad_lanes(fg)], axis=1).astype(BF16)
            proj, gates = norm_proj(h, norm_mix[layer], w_main, w_gate, EVEN_PROJ_TILE)
            ccol, crow = fox_gates(gates, _pad_lanes(fox_fgate_bias[e].reshape(1, -1)), batch,
                                   HGRN_WIDTH // LANES)
            a_out = fox_attention(proj, ccol, crow, batch)
            blocks = FOX_WIDTH // LANES
            b_out = hgrn2(proj, gates, hgrn_lb_logits, hgrn_out_norm[e], batch, e,
                          3 * blocks, 3 * blocks + HGRN_HEADS, 3 * blocks + 2 * HGRN_HEADS)
            w_out = ev_out_proj[e].astype(BF16)
            mixes, w_outs = [a_out, b_out], [w_out[:FOX_WIDTH], w_out[FOX_WIDTH:]]
        else:
            o = layer // 2
            w_in = ssm_in_proj[o]
            n_main = 2 * SSM_INNER + 2 * SSM_BC
            proj, dt_raw = norm_proj(h, norm_mix[layer], w_in[:, :n_main].astype(BF16),
                                     _pad_lanes(w_in[:, n_main:]).astype(BF16), SSM_PROJ_TILE)
            y = ssd_mixer(proj, dt_raw, ssm_conv_w[o], ssm_conv_b[o].reshape(1, -1),
                          _pad_lanes(ssm_dt_bias[o].reshape(1, -1)), _pad_lanes(ssm_A_log[o].reshape(1, -1)),
                          jnp.repeat(ssm_D[o], SSM_HEAD_DIM).reshape(1, -1), ssm_norm[o].reshape(1, -1), batch)
            mixes, w_outs = [y], [ssm_out_proj[o].astype(BF16)]

        kv = norm_matmul(mem2, mem_norm, xa_kv[layer].astype(BF16), BF16, 2 * XATTN_WIDTH)
        h = post_mixer(h, mixes, w_outs, norm_xattn[layer], xa_q[layer].astype(BF16), kv,
                       xa_o[layer].astype(BF16), norm_mlp[layer], mlp_up[layer].astype(BF16),
                       mlp_down[layer].astype(BF16), norm_final, layer == depth - 1, batch)

    return h.reshape(batch, seq, d)
```

```python
import functools

import jax
import jax.numpy as jnp
from jax import lax
from jax.experimental import pallas as pl
from jax.experimental.pallas import tpu as pltpu

F32 = jnp.float32
BF16 = jnp.bfloat16
HIGHEST = lax.Precision.HIGHEST

NORM_EPS = 1e-6
MASK_VALUE = -1e30
LOG_FLOOR = 1e-30
LOG2E = 1.4426950408889634

LANES = 128
VMEM_LIMIT_BYTES = 56 * 1024 * 1024

FOX_HEADS = 8
FOX_HEAD_DIM = 64
FOX_WIDTH = FOX_HEADS * FOX_HEAD_DIM
FOX_PAIRS = FOX_WIDTH // LANES
FOX_TILE = 256
FOX_TQ = 512
FOX_TK = 512

HGRN_HEADS = 4
HGRN_DIM = 128
HGRN_WIDTH = HGRN_HEADS * HGRN_DIM
HGRN_CHUNK = 64
HGRN_SUB = 16
HGRN_ROWS = 1024

SSM_INNER = 2048
SSM_HEAD_DIM = 64
SSM_HEADS = SSM_INNER // SSM_HEAD_DIM
SSM_GROUPS = 8
SSM_GROUP_WIDTH = SSM_INNER // SSM_GROUPS
SSM_STATE = 128
SSM_CONV = 4
SSM_CHUNK = 128
SSM_ROWS = 512
SSM_BC = SSM_GROUPS * SSM_STATE

XATTN_HEADS = 4
XATTN_HEAD_DIM = 128
XATTN_WIDTH = XATTN_HEADS * XATTN_HEAD_DIM

ROW_TILE = 512
MLP_ROW_TILE = 1024
MLP_FF_TILE = 1024
EVEN_PROJ_TILE = 1536
SSM_PROJ_TILE = 2048


def _params(*semantics):
    return pltpu.CompilerParams(dimension_semantics=semantics, vmem_limit_bytes=VMEM_LIMIT_BYTES)


def _rms(x, gain):
    ms = jnp.mean(x * x, axis=-1, keepdims=True)
    return x * lax.rsqrt(ms + NORM_EPS) * gain


def _log1p_exp_neg(x):
    return jnp.log(1.0 + jnp.exp(-x))


def _log_sigmoid(x):
    return jnp.minimum(x, 0.0) - _log1p_exp_neg(jnp.abs(x))


def _sigmoid(x):
    return 1.0 / (1.0 + jnp.exp(-x))


def _softplus(x):
    return jnp.maximum(x, 0.0) + _log1p_exp_neg(jnp.abs(x))


def _dot(a, b):
    return jnp.dot(a, b, preferred_element_type=F32)


def _dot_nt(a, b):
    return lax.dot_general(a, b, (((1,), (1,)), ((), ())), preferred_element_type=F32)


def _cumsum_rows(x):
    n = x.shape[0]
    r = lax.broadcasted_iota(jnp.int32, (n, n), 0)
    c = lax.broadcasted_iota(jnp.int32, (n, n), 1)
    tri = jnp.where(r >= c, 1.0, 0.0).astype(F32)
    return jnp.dot(tri, x, precision=HIGHEST, preferred_element_type=F32)


def _norm_matmul_kernel(h_ref, g_ref, w_ref, o_ref, hn_ref):
    @pl.when(pl.program_id(1) == 0)
    def _():
        hn_ref[...] = _rms(h_ref[...], g_ref[...]).astype(BF16)

    o_ref[...] = _dot(hn_ref[...], w_ref[...]).astype(o_ref.dtype)


def norm_matmul(h, gain, w, out_dtype, tn):
    rows, d = h.shape
    n = w.shape[1]
    tm = min(ROW_TILE * 2, rows)
    return pl.pallas_call(
        _norm_matmul_kernel,
        grid=(rows // tm, n // tn),
        in_specs=[pl.BlockSpec((tm, d), lambda i, j: (i, 0)),
                  pl.BlockSpec((1, d), lambda i, j: (0, 0)),
                  pl.BlockSpec((d, tn), lambda i, j: (0, j))],
        out_specs=pl.BlockSpec((tm, tn), lambda i, j: (i, j)),
        out_shape=jax.ShapeDtypeStruct((rows, n), out_dtype),
        scratch_shapes=[pltpu.VMEM((tm, d), BF16)],
        compiler_params=_params("parallel", "arbitrary"),
        name="norm_matmul",
    )(h, gain.reshape(1, d), w)


def _norm_proj_kernel(h_ref, g_ref, w_ref, wg_ref, o_ref, og_ref, hn_ref):
    @pl.when(pl.program_id(1) == 0)
    def _():
        hn = _rms(h_ref[...], g_ref[...]).astype(BF16)
        hn_ref[...] = hn
        og_ref[...] = _dot(hn, wg_ref[...])

    o_ref[...] = _dot(hn_ref[...], w_ref[...]).astype(o_ref.dtype)


def norm_proj(h, gain, w, w_gate, tn):
    rows, d = h.shape
    n, ng = w.shape[1], w_gate.shape[1]
    tm = min(ROW_TILE * 2, rows)
    return pl.pallas_call(
        _norm_proj_kernel,
        grid=(rows // tm, n // tn),
        in_specs=[pl.BlockSpec((tm, d), lambda i, j: (i, 0)),
                  pl.BlockSpec((1, d), lambda i, j: (0, 0)),
                  pl.BlockSpec((d, tn), lambda i, j: (0, j)),
                  pl.BlockSpec((d, ng), lambda i, j: (0, 0))],
        out_specs=[pl.BlockSpec((tm, tn), lambda i, j: (i, j)),
                   pl.BlockSpec((tm, ng), lambda i, j: (i, 0))],
        out_shape=[jax.ShapeDtypeStruct((rows, n), BF16), jax.ShapeDtypeStruct((rows, ng), F32)],
        scratch_shapes=[pltpu.VMEM((tm, d), BF16)],
        compiler_params=_params("parallel", "arbitrary"),
        name="norm_proj",
    )(h, gain.reshape(1, d), w, w_gate)


def _cross_attention(x, gain, wq_ref, kv_ref, wo_ref):
    hn = _rms(x, gain).astype(BF16)
    q = _dot(hn, wq_ref[...]).astype(BF16)
    scale = XATTN_HEAD_DIM ** -0.5
    outs = []
    for hd in range(XATTN_HEADS):
        lo = hd * XATTN_HEAD_DIM
        qh = q[:, lo:lo + XATTN_HEAD_DIM]
        kh = kv_ref[:, lo:lo + XATTN_HEAD_DIM]
        vh = kv_ref[:, XATTN_WIDTH + lo:XATTN_WIDTH + lo + XATTN_HEAD_DIM]
        s = _dot_nt(qh, kh) * scale
        p = jnp.exp(s - jnp.max(s, axis=-1, keepdims=True))
        denom = jnp.sum(p, axis=-1, keepdims=True)
        outs.append((_dot(p.astype(BF16), vh) / denom).astype(BF16))
    return _dot(jnp.concatenate(outs, axis=-1), wo_ref[...])


def _post_mixer_kernel(*refs, n_mix, final_norm):
    mix_refs, wout_refs = refs[:n_mix], refs[n_mix:2 * n_mix]
    (h_ref, gx_ref, wq_ref, kv_ref, wo_ref, gm_ref, wu_ref, wd_ref, gf_ref,
     o_ref, hn_ref) = refs[2 * n_mix:]
    f = pl.program_id(1)

    @pl.when(f == 0)
    def _():
        x = h_ref[...]
        for m_ref, w_ref in zip(mix_refs, wout_refs):
            x = x + _dot(m_ref[...], w_ref[...])
        x = x + _cross_attention(x, gx_ref[...], wq_ref, kv_ref, wo_ref)
        hn_ref[...] = _rms(x, gm_ref[...]).astype(BF16)
        o_ref[...] = x

    u = _dot(hn_ref[...], wu_ref[...])
    u = jnp.square(jnp.maximum(u, 0.0)).astype(BF16)
    o_ref[...] += _dot(u, wd_ref[...])

    if final_norm:
        @pl.when(f == pl.num_programs(1) - 1)
        def _():
            o_ref[...] = _rms(o_ref[...], gf_ref[...])


def post_mixer(h, mixes, w_outs, gain_x, wq, kv, wo, gain_mlp, w_up, w_down, final_gain, final_norm, batch):
    rows, d = h.shape
    seq = rows // batch
    mem_len = kv.shape[0] // batch
    ff = w_up.shape[1]
    tm = min(MLP_ROW_TILE, seq)
    tf = min(MLP_FF_TILE, ff)
    tiles_per_batch = seq // tm

    def const(shape):
        return pl.BlockSpec(shape, lambda i, f: (0, 0))

    in_specs = [pl.BlockSpec((tm, m.shape[1]), lambda i, f: (i, 0)) for m in mixes]
    in_specs += [const(w.shape) for w in w_outs]
    in_specs += [pl.BlockSpec((tm, d), lambda i, f: (i, 0)), const((1, d)), const(wq.shape),
                 pl.BlockSpec((mem_len, 2 * XATTN_WIDTH), lambda i, f: (i // tiles_per_batch, 0)),
                 const(wo.shape), const((1, d)),
                 pl.BlockSpec((d, tf), lambda i, f: (0, f)),
                 pl.BlockSpec((tf, d), lambda i, f: (f, 0)),
                 const((1, d))]
    return pl.pallas_call(
        functools.partial(_post_mixer_kernel, n_mix=len(mixes), final_norm=final_norm),
        grid=(rows // tm, ff // tf),
        in_specs=in_specs,
        out_specs=pl.BlockSpec((tm, d), lambda i, f: (i, 0)),
        out_shape=jax.ShapeDtypeStruct((rows, d), F32),
        scratch_shapes=[pltpu.VMEM((tm, d), BF16)],
        compiler_params=_params("parallel", "arbitrary"),
        name="post_mixer",
    )(*mixes, *w_outs, h, gain_x.reshape(1, d), wq, kv, wo, gain_mlp.reshape(1, d), w_up, w_down,
      final_gain.reshape(1, d))


def _fox_gate_kernel(fg_ref, bias_ref, ccol_ref, crow_ref):
    seq = fg_ref.shape[0]
    blk = FOX_TILE
    carry = jnp.zeros((1, LANES), F32)
    for i in range(seq // blk):
        rows = slice(i * blk, (i + 1) * blk)
        c = _cumsum_rows(_log_sigmoid(fg_ref[rows, :] + bias_ref[...])) + carry
        c2 = c * LOG2E
        ccol_ref[rows, :] = c2
        crow_ref[0, :, rows] = c2.T[:FOX_HEADS, :]
        carry = c[blk - 1:blk, :]


def fox_gates(gates, bias, batch, col_block):
    rows = gates.shape[0]
    seq = rows // batch
    return pl.pallas_call(
        _fox_gate_kernel,
        grid=(batch,),
        in_specs=[pl.BlockSpec((seq, LANES), lambda b: (b, col_block)),
                  pl.BlockSpec((1, LANES), lambda b: (0, 0))],
        out_specs=[pl.BlockSpec((seq, LANES), lambda b: (b, 0)),
                   pl.BlockSpec((1, FOX_HEADS, seq), lambda b: (b, 0, 0))],
        out_shape=[jax.ShapeDtypeStruct((rows, LANES), F32),
                   jax.ShapeDtypeStruct((batch, FOX_HEADS, seq), F32)],
        compiler_params=_params("parallel"),
        name="fox_gates",
    )(gates, bias)


FOX_ONES_ROWS = 16


def _fox_kernel(q_ref, k_ref, v_ref, ccol_ref, crow_ref, o_ref, vt_ref, cs_ref, st_ref, p_ref, alpha_ref,
                acc_ref, m_ref):
    pair = pl.program_id(1)
    tq, tk = FOX_TQ, FOX_TK
    seq = k_ref.shape[0]
    hd = FOX_HEAD_DIM

    lane = lax.broadcasted_iota(jnp.int32, (tk, LANES), 1)
    for blk in range(seq // tk):
        rows = slice(blk * tk, (blk + 1) * tk)
        vt = v_ref[rows, :].astype(F32).T
        c = ccol_ref[rows, :]
        for hh in range(2):
            vt_ref[hh, 0:hd, rows] = vt[hh * hd:(hh + 1) * hd, :].astype(BF16)
            vt_ref[hh, hd:hd + FOX_ONES_ROWS, rows] = jnp.ones((FOX_ONES_ROWS, tk), BF16)
            col = jnp.sum(jnp.where(lane == 2 * pair + hh, c, 0.0), axis=-1, keepdims=True)
            cs_ref[hh, rows, :] = jnp.broadcast_to(col, (tk, LANES))

    row = lax.broadcasted_iota(jnp.int32, (LANES, tq), 0)
    key_minus_query = (lax.broadcasted_iota(jnp.int32, (tk, tq), 0)
                       - lax.broadcasted_iota(jnp.int32, (tk, tq), 1))

    def query_tile(qi):
        slot = qi % 2
        q_start = qi * tq
        n_kv = -(-(q_start + tq) // tk)
        qt = q_ref[q_start:q_start + tq, :].astype(F32).T
        qts = (jnp.where(row < hd, qt, 0.0).astype(BF16), jnp.where(row < hd, 0.0, qt).astype(BF16))
        cts = [crow_ref[0, pl.ds(2 * pair + hh, 1), q_start:q_start + tq] for hh in range(2)]

        def scores(j):
            k = k_ref[j * tk:(j + 1) * tk, :]
            return [_dot(k, qts[hh]) for hh in range(2)]

        def values_times_probs(j):
            return [_dot(vt_ref[hh, :, j * tk:(j + 1) * tk], p_ref[slot, j % 2, hh]) for hh in range(2)]

        first_scores = scores(0)
        for hh in range(2):
            st_ref[slot, 0, hh] = first_scores[hh]
            m_ref[slot, hh] = jnp.full(m_ref.shape[2:], MASK_VALUE, F32)

        for j in range(n_kv):
            par, prev = j % 2, 1 - j % 2
            if j > 0:
                pv = values_times_probs(j - 1)
            if j + 1 < n_kv:
                next_scores = scores(j + 1)
            for hh in range(2):
                cs = cs_ref[hh, j * tk:(j + 1) * tk, :]
                st = st_ref[slot, par, hh] + (cts[hh] - jnp.concatenate([cs] * (tq // LANES), axis=1))
                if (j + 1) * tk - 1 > q_start:
                    st = jnp.where(key_minus_query <= q_start - j * tk, st, MASK_VALUE)
                m_old = m_ref[slot, hh]
                m_new = jnp.maximum(m_old, jnp.max(st, axis=0, keepdims=True))
                p_ref[slot, par, hh] = jnp.exp2(st - m_new).astype(BF16)
                if j == 1:
                    acc_ref[slot, hh] = pv[hh]
                elif j > 1:
                    acc_ref[slot, hh] = alpha_ref[slot, prev, hh] * acc_ref[slot, hh] + pv[hh]
                alpha_ref[slot, par, hh] = jnp.exp2(m_old - m_new)
                m_ref[slot, hh] = m_new
            if j + 1 < n_kv:
                for hh in range(2):
                    st_ref[slot, prev, hh] = next_scores[hh]

        last = n_kv - 1
        pv = values_times_probs(last)
        outs = []
        for hh in range(2):
            acc = pv[hh] if last == 0 else alpha_ref[slot, last % 2, hh] * acc_ref[slot, hh] + pv[hh]
            outs.append(acc[0:hd, :] / acc[hd:hd + 1, :])
        o_ref[q_start:q_start + tq, :] = jnp.concatenate(outs, axis=0).T.astype(o_ref.dtype)

    for qi in range(seq // tq):
        query_tile(qi)


def fox_attention(proj, ccol, crow, batch):
    rows = proj.shape[0]
    seq = rows // batch
    tq, tk = FOX_TQ, FOX_TK
    acc_rows = FOX_HEAD_DIM + FOX_ONES_ROWS
    return pl.pallas_call(
        _fox_kernel,
        grid=(batch, FOX_PAIRS),
        in_specs=[pl.BlockSpec((seq, LANES), lambda b, p: (b, p)),
                  pl.BlockSpec((seq, LANES), lambda b, p: (b, FOX_PAIRS + p)),
                  pl.BlockSpec((seq, LANES), lambda b, p: (b, 2 * FOX_PAIRS + p)),
                  pl.BlockSpec((seq, LANES), lambda b, p: (b, 0)),
                  pl.BlockSpec((1, FOX_HEADS, seq), lambda b, p: (b, 0, 0))],
        out_specs=pl.BlockSpec((seq, LANES), lambda b, p: (b, p)),
        out_shape=jax.ShapeDtypeStruct((rows, FOX_WIDTH), BF16),
        scratch_shapes=[pltpu.VMEM((2, acc_rows, seq), BF16), pltpu.VMEM((2, seq, LANES), F32),
                        pltpu.VMEM((2, 2, 2, tk, tq), F32), pltpu.VMEM((2, 2, 2, tk, tq), BF16),
                        pltpu.VMEM((2, 2, 2, 1, tq), F32),
                        pltpu.VMEM((2, 2, acc_rows, tq), F32), pltpu.VMEM((2, 2, 1, tq), F32)],
        compiler_params=_params("parallel", "parallel"),
        name="fox_attention",
    )(proj, proj, proj, ccol, crow)


def _hgrn_kernel(hq_ref, hi_ref, hg_ref, hf_ref, lbl_ref, gain_ref, o_ref, state_ref, *, layer):
    @pl.when(pl.program_id(2) == 0)
    def _():
        state_ref[...] = jnp.zeros(state_ref.shape, F32)

    logits = lbl_ref[...]
    w = jnp.exp(logits - jnp.max(logits, axis=0, keepdims=True))
    w = w / jnp.sum(w, axis=0, keepdims=True)
    lb = jnp.sum(w[:layer + 1], axis=0, keepdims=True) - w[0:1]
    log_lb = jnp.log(jnp.maximum(lb, LOG_FLOOR))
    log1m_lb = jnp.log1p(-lb)

    ch, sub = HGRN_CHUNK, HGRN_SUB
    n_sub = ch // sub
    lane_c = lax.broadcasted_iota(jnp.int32, (sub, ch), 1)
    row_c = lax.broadcasted_iota(jnp.int32, (sub, ch), 0)

    def chunk(ci):
        rows = slice(ci * ch, (ci + 1) * ch)
        hf = hf_ref[rows, :]
        hq = hq_ref[rows, :].astype(F32)
        v = hi_ref[rows, :].astype(F32)
        hg = hg_ref[rows, :].astype(F32)

        log_gate = log1m_lb + _log_sigmoid(hf)
        log_f = jnp.maximum(log_lb, log_gate) + _log1p_exp_neg(jnp.abs(log_lb - log_gate))
        k = (1.0 - lb) * _sigmoid(-hf)
        q = hq * _sigmoid(hq)
        b = _cumsum_rows(log_f) * LOG2E

        bref = jnp.concatenate(
            [jnp.broadcast_to(b[max(i * sub - 1, 0):max(i * sub - 1, 0) + 1, :], (sub, HGRN_DIM))
             for i in range(n_sub)], axis=0)
        q_rel = (q * jnp.exp2(b - bref)).astype(BF16)

        a_rows = []
        for i in range(n_sub):
            s0 = i * sub
            bq = b[s0:s0 + sub, :]
            qq = q[s0:s0 + sub, :]
            diag = jnp.zeros((sub, ch), F32)
            for s in range(sub):
                bs = b[s0 + s:s0 + s + 1, :]
                ks = k[s0 + s:s0 + s + 1, :]
                e = jnp.exp2(bq - bs)
                col = jnp.sum(qq * ks * e, axis=-1, keepdims=True)
                diag = jnp.where(lane_c == s0 + s, col, diag)
            diag = jnp.where(row_c + s0 >= lane_c, diag, 0.0)
            if i == 0:
                a_rows.append(diag)
            else:
                k_rel = (k * jnp.exp2(jnp.minimum(bref[s0:s0 + 1, :] - b, 0.0))).astype(BF16)
                off = _dot_nt(q_rel[s0:s0 + sub, :], k_rel)
                a_rows.append(jnp.where(lane_c < s0, off, diag))
        a = jnp.concatenate(a_rows, axis=0).astype(BF16)

        state_t = state_ref[...]
        b_last = b[ch - 1:ch, :]
        out = _dot(a, v.astype(BF16)) + _dot_nt((q * jnp.exp2(b)).astype(BF16), state_t.astype(BF16))
        k_end = (k * jnp.exp2(b_last - b)).astype(BF16)
        state_ref[...] = state_t * jnp.exp2(b_last) + _dot(v.T.astype(BF16), k_end)

        y = out * lax.rsqrt(jnp.mean(out * out, axis=-1, keepdims=True) + NORM_EPS) * gain_ref[...]
        o_ref[rows, :] = (y * (hg * _sigmoid(hg))).astype(o_ref.dtype)

    for ci in range(o_ref.shape[0] // ch):
        chunk(ci)


def hgrn2(proj, gates, lb_logits, gain, batch, layer, col_q, col_i, col_g):
    rows = proj.shape[0]
    seq = rows // batch
    tr = min(HGRN_ROWS, seq)
    nt = seq // tr
    n_layers = lb_logits.shape[0]

    def tok(col0):
        return pl.BlockSpec((tr, LANES), lambda b, h, t: (b * nt + t, col0 + h))

    return pl.pallas_call(
        functools.partial(_hgrn_kernel, layer=layer),
        grid=(batch, HGRN_HEADS, nt),
        in_specs=[tok(col_q), tok(col_i), tok(col_g), tok(0),
                  pl.BlockSpec((n_layers, LANES), lambda b, h, t: (0, h)),
                  pl.BlockSpec((1, LANES), lambda b, h, t: (0, h))],
        out_specs=tok(0),
        out_shape=jax.ShapeDtypeStruct((rows, HGRN_WIDTH), BF16),
        scratch_shapes=[pltpu.VMEM((HGRN_DIM, HGRN_DIM), F32)],
        compiler_params=_params("parallel", "parallel", "arbitrary"),
        name="hgrn2",
    )(proj, proj, proj, gates, lb_logits, gain.reshape(1, HGRN_WIDTH))


def _causal_conv_silu(x, tail, w, bias):
    xx = jnp.concatenate([tail, x], axis=0)
    acc = xx * w[0:1, :]
    for j in range(1, SSM_CONV):
        acc = pltpu.roll(acc, 1, axis=0) + xx * w[j:j + 1, :]
    acc = acc[tail.shape[0]:, :] + bias
    return acc * _sigmoid(acc)


def _pair_select(first, a, b):
    return jnp.where(first, a, b)


def _ssd_kernel(z_ref, x_ref, bc_ref, dt_ref, cwx_ref, cwbc_ref, cbx_ref, cbbc_ref, dtb_ref, alog_ref,
                dskip_ref, gain_ref, o_ref, state_ref, tailx_ref, tailbc_ref):
    @pl.when(pl.program_id(1) == 0)
    def _():
        state_ref[...] = jnp.zeros(state_ref.shape, F32)
        tailx_ref[...] = jnp.zeros(tailx_ref.shape, F32)
        tailbc_ref[...] = jnp.zeros(tailbc_ref.shape, F32)

    ch = SSM_CHUNK
    a2 = -jnp.exp(alog_ref[...]) * LOG2E
    r = lax.broadcasted_iota(jnp.int32, (ch, ch), 0)
    c = lax.broadcasted_iota(jnp.int32, (ch, ch), 1)
    causal = r >= c
    first = lax.broadcasted_iota(jnp.int32, (ch, LANES), 1) < SSM_HEAD_DIM
    first_row = first[0:1, :]

    def bcast(col):
        return jnp.broadcast_to(col, (ch, LANES))

    def conv(ref, tail_ref, cw_ref, cb_ref, rows, cols):
        raw = ref[rows, cols].astype(F32)
        out = _causal_conv_silu(raw, tail_ref[:, cols], cw_ref[:, cols], cb_ref[:, cols])
        tail_ref[:, cols] = raw[ch - 8:, :]
        return out

    def chunk(rows):
        dt = _softplus(dt_ref[rows, :] + dtb_ref[...])
        cum = _cumsum_rows(dt * a2)
        cum_t = cum.T
        dt_t = dt.T
        exp_cum = jnp.exp2(cum)
        w_end = jnp.exp2(cum[ch - 1:ch, :] - cum) * dt

        for g in range(SSM_GROUPS):
            gx = slice(g * SSM_GROUP_WIDTH, (g + 1) * SSM_GROUP_WIDTH)
            gb = slice(g * SSM_STATE, (g + 1) * SSM_STATE)
            gc = slice(SSM_BC + g * SSM_STATE, SSM_BC + (g + 1) * SSM_STATE)
            xs = conv(x_ref, tailx_ref, cwx_ref, cbx_ref, rows, gx)
            bm = conv(bc_ref, tailbc_ref, cwbc_ref, cbbc_ref, rows, gb)
            cm = conv(bc_ref, tailbc_ref, cwbc_ref, cbbc_ref, rows, gc)
            bm16, cm16 = bm.astype(BF16), cm.astype(BF16)
            cb = _dot_nt(cm16, bm16)
            state = state_ref[g]
            y_state = _dot(cm16, state.astype(BF16))

            y_pairs, wx_pairs, decay_pairs = [], [], []
            for pr in range(2):
                xp = xs[:, pr * LANES:(pr + 1) * LANES]
                xp16 = xp.astype(BF16)
                ys, scale_in, scale_w, ends = [], [], [], []
                for hh in range(2):
                    h = g * 4 + pr * 2 + hh
                    ccol = bcast(cum[:, h:h + 1])
                    seg = jnp.where(causal, ccol - cum_t[h:h + 1, :], MASK_VALUE)
                    m = (cb * jnp.exp2(seg) * dt_t[h:h + 1, :]).astype(BF16)
                    ys.append(_dot(m, xp16))
                    scale_in.append(bcast(exp_cum[:, h:h + 1]))
                    scale_w.append(bcast(w_end[:, h:h + 1]))
                    ends.append(ccol[ch - 1:ch, :])
                y_in = y_state[:, pr * LANES:(pr + 1) * LANES] * _pair_select(first, scale_in[0], scale_in[1])
                y_pairs.append(_pair_select(first, ys[0], ys[1]) + y_in)
                wx_pairs.append((xp * _pair_select(first, scale_w[0], scale_w[1])).astype(BF16))
                decay_pairs.append(jnp.exp2(_pair_select(first_row, ends[0], ends[1])))
            y = jnp.concatenate(y_pairs, axis=-1)
            wx = jnp.concatenate(wx_pairs, axis=-1)
            state_ref[g] = state * jnp.concatenate(decay_pairs, axis=-1) + _dot(bm.T.astype(BF16), wx)

            y = y + dskip_ref[:, gx] * xs
            zg = z_ref[rows, gx].astype(F32)
            y = y * (zg * _sigmoid(zg))
            y = y * lax.rsqrt(jnp.mean(y * y, axis=-1, keepdims=True) + NORM_EPS) * gain_ref[:, gx]
            o_ref[rows, gx] = y.astype(o_ref.dtype)

    for ci in range(o_ref.shape[0] // ch):
        chunk(slice(ci * ch, (ci + 1) * ch))


def ssd_mixer(proj, dt_raw, conv_w, conv_b, dt_bias, a_log, d_skip, gain, batch):
    rows = proj.shape[0]
    seq = rows // batch
    tr = min(SSM_ROWS, seq)
    nt = seq // tr

    def tok(width, col):
        return pl.BlockSpec((tr, width), lambda b, c: (b * nt + c, col))

    def par(nrows, width, col):
        return pl.BlockSpec((nrows, width), lambda b, c: (0, col))

    return pl.pallas_call(
        _ssd_kernel,
        grid=(batch, nt),
        in_specs=[tok(SSM_INNER, 0), tok(SSM_INNER, 1), tok(2 * SSM_BC, 2), tok(LANES, 0),
                  par(SSM_CONV, SSM_INNER, 0), par(SSM_CONV, 2 * SSM_BC, 1),
                  par(1, SSM_INNER, 0), par(1, 2 * SSM_BC, 1),
                  par(1, LANES, 0), par(1, LANES, 0), par(1, SSM_INNER, 0), par(1, SSM_INNER, 0)],
        out_specs=tok(SSM_INNER, 0),
        out_shape=jax.ShapeDtypeStruct((rows, SSM_INNER), BF16),
        scratch_shapes=[pltpu.VMEM((SSM_GROUPS, SSM_STATE, SSM_GROUP_WIDTH), F32),
                        pltpu.VMEM((8, SSM_INNER), F32), pltpu.VMEM((8, 2 * SSM_BC), F32)],
        compiler_params=_params("parallel", "arbitrary"),
        name="ssd_mixer",
    )(proj, proj, proj, dt_raw, conv_w, conv_w, conv_b, conv_b, dt_bias, a_log, d_skip, gain)


def _pad_lanes(x):
    return jnp.pad(x, ((0, 0), (0, LANES - x.shape[1])))


def _split_cols(w, sizes):
    out, lo = [], 0
    for s in sizes:
        out.append(w[:, lo:lo + s])
        lo += s
    return out


def kernel(x, mem, mem_norm, norm_mix, norm_xattn, norm_mlp, norm_final, ev_in_proj, fox_fgate_bias,
           hgrn_lb_logits, hgrn_out_norm, ev_out_proj, ssm_in_proj, ssm_conv_w, ssm_conv_b, ssm_dt_bias,
           ssm_A_log, ssm_D, ssm_norm, ssm_out_proj, xa_q, xa_kv, xa_o, mlp_up, mlp_down):
    batch, seq, d = x.shape
    depth = norm_mix.shape[0]
    h = x.reshape(batch * seq, d)
    mem2 = mem.reshape(-1, d)

    for layer in range(depth):
        if layer % 2 == 0:
            e = layer // 2
            fq, fk, fv, fg, hq, hf, hi, hg = _split_cols(
                ev_in_proj[e], (FOX_WIDTH, FOX_WIDTH, FOX_WIDTH, FOX_HEADS,
                                HGRN_WIDTH, HGRN_WIDTH, HGRN_WIDTH, HGRN_WIDTH))
            w_main = jnp.concatenate([fq * (FOX_HEAD_DIM ** -0.5 * LOG2E), fk, fv, hq, hi, hg], axis=1).astype(BF16)
            w_gate = jnp.concatenate([hf, _pad_lanes(fg)], axis=1).astype(BF16)
            proj, gates = norm_proj(h, norm_mix[layer], w_main, w_gate, EVEN_PROJ_TILE)
            ccol, crow = fox_gates(gates, _pad_lanes(fox_fgate_bias[e].reshape(1, -1)), batch,
                                   HGRN_WIDTH // LANES)
            a_out = fox_attention(proj, ccol, crow, batch)
            blocks = FOX_WIDTH // LANES
            b_out = hgrn2(proj, gates, hgrn_lb_logits, hgrn_out_norm[e], batch, e,
                          3 * blocks, 3 * blocks + HGRN_HEADS, 3 * blocks + 2 * HGRN_HEADS)
            w_out = ev_out_proj[e].astype(BF16)
            mixes, w_outs = [a_out, b_out], [w_out[:FOX_WIDTH], w_out[FOX_WIDTH:]]
        else:
            o = layer // 2
            w_in = ssm_in_proj[o]
            n_main = 2 * SSM_INNER + 2 * SSM_BC
            proj, dt_raw = norm_proj(h, norm_mix[layer], w_in[:, :n_main].astype(BF16),
                                     _pad_lanes(w_in[:, n_main:]).astype(BF16), SSM_PROJ_TILE)
            y = ssd_mixer(proj, dt_raw, ssm_conv_w[o], ssm_conv_b[o].reshape(1, -1),
                          _pad_lanes(ssm_dt_bias[o].reshape(1, -1)), _pad_lanes(ssm_A_log[o].reshape(1, -1)),
                          jnp.repeat(ssm_D[o], SSM_HEAD_DIM).reshape(1, -1), ssm_norm[o].reshape(1, -1), batch)
            mixes, w_outs = [y], [ssm_out_proj[o].astype(BF16)]

        kv = norm_matmul(mem2, mem_norm, xa_kv[layer].astype(BF16), BF16, 2 * XATTN_WIDTH)
        h = post_mixer(h, mixes, w_outs, norm_xattn[layer], xa_q[layer].astype(BF16), kv,
                       xa_o[layer].astype(BF16), norm_mlp[layer], mlp_up[layer].astype(BF16),
                       mlp_down[layer].astype(BF16), norm_final, layer == depth - 1, batch)

    return h.reshape(batch, seq, d)
```

```python
import functools

import jax
import jax.numpy as jnp
from jax import lax
from jax.experimental import pallas as pl
from jax.experimental.pallas import tpu as pltpu

F32 = jnp.float32
BF16 = jnp.bfloat16
HIGHEST = lax.Precision.HIGHEST

NORM_EPS = 1e-6
MASK_VALUE = -1e30
LOG_FLOOR = 1e-30
LOG2E = 1.4426950408889634

LANES = 128
VMEM_LIMIT_BYTES = 56 * 1024 * 1024

FOX_HEADS = 8
FOX_HEAD_DIM = 64
FOX_WIDTH = FOX_HEADS * FOX_HEAD_DIM
FOX_PAIRS = FOX_WIDTH // LANES
FOX_TILE = 256
FOX_TQ = 512
FOX_TK = 512

HGRN_HEADS = 4
HGRN_DIM = 128
HGRN_WIDTH = HGRN_HEADS * HGRN_DIM
HGRN_CHUNK = 64
HGRN_SUB = 16
HGRN_ROWS = 1024

SSM_INNER = 2048
SSM_HEAD_DIM = 64
SSM_HEADS = SSM_INNER // SSM_HEAD_DIM
SSM_GROUPS = 8
SSM_GROUP_WIDTH = SSM_INNER // SSM_GROUPS
SSM_STATE = 128
SSM_CONV = 4
SSM_CHUNK = 128
SSM_ROWS = 512
SSM_BC = SSM_GROUPS * SSM_STATE

XATTN_HEADS = 4
XATTN_HEAD_DIM = 128
XATTN_WIDTH = XATTN_HEADS * XATTN_HEAD_DIM

ROW_TILE = 512
MLP_ROW_TILE = 1024
MLP_FF_TILE = 1024
EVEN_PROJ_TILE = 1536
SSM_PROJ_TILE = 2048


def _params(*semantics):
    return pltpu.CompilerParams(dimension_semantics=semantics, vmem_limit_bytes=VMEM_LIMIT_BYTES)


def _rms(x, gain):
    ms = jnp.mean(x * x, axis=-1, keepdims=True)
    return x * lax.rsqrt(ms + NORM_EPS) * gain


def _log1p_exp_neg(x):
    return jnp.log(1.0 + jnp.exp(-x))


def _log_sigmoid(x):
    return jnp.minimum(x, 0.0) - _log1p_exp_neg(jnp.abs(x))


def _sigmoid(x):
    return 1.0 / (1.0 + jnp.exp(-x))


def _softplus(x):
    return jnp.maximum(x, 0.0) + _log1p_exp_neg(jnp.abs(x))


def _dot(a, b):
    return jnp.dot(a, b, preferred_element_type=F32)


def _dot_nt(a, b):
    return lax.dot_general(a, b, (((1,), (1,)), ((), ())), preferred_element_type=F32)


def _cumsum_rows(x):
    n = x.shape[0]
    r = lax.broadcasted_iota(jnp.int32, (n, n), 0)
    c = lax.broadcasted_iota(jnp.int32, (n, n), 1)
    tri = jnp.where(r >= c, 1.0, 0.0).astype(F32)
    return jnp.dot(tri, x, precision=HIGHEST, preferred_element_type=F32)


def _norm_matmul_kernel(h_ref, g_ref, w_ref, o_ref, hn_ref):
    @pl.when(pl.program_id(1) == 0)
    def _():
        hn_ref[...] = _rms(h_ref[...], g_ref[...]).astype(BF16)

    o_ref[...] = _dot(hn_ref[...], w_ref[...]).astype(o_ref.dtype)


def norm_matmul(h, gain, w, out_dtype, tn):
    rows, d = h.shape
    n = w.shape[1]
    tm = min(ROW_TILE * 2, rows)
    return pl.pallas_call(
        _norm_matmul_kernel,
        grid=(rows // tm, n // tn),
        in_specs=[pl.BlockSpec((tm, d), lambda i, j: (i, 0)),
                  pl.BlockSpec((1, d), lambda i, j: (0, 0)),
                  pl.BlockSpec((d, tn), lambda i, j: (0, j))],
        out_specs=pl.BlockSpec((tm, tn), lambda i, j: (i, j)),
        out_shape=jax.ShapeDtypeStruct((rows, n), out_dtype),
        scratch_shapes=[pltpu.VMEM((tm, d), BF16)],
        compiler_params=_params("parallel", "arbitrary"),
        name="norm_matmul",
    )(h, gain.reshape(1, d), w)


def _norm_proj_kernel(h_ref, g_ref, w_ref, wg_ref, o_ref, og_ref, hn_ref):
    @pl.when(pl.program_id(1) == 0)
    def _():
        hn = _rms(h_ref[...], g_ref[...]).astype(BF16)
        hn_ref[...] = hn
        og_ref[...] = _dot(hn, wg_ref[...])

    o_ref[...] = _dot(hn_ref[...], w_ref[...]).astype(o_ref.dtype)


def norm_proj(h, gain, w, w_gate, tn):
    rows, d = h.shape
    n, ng = w.shape[1], w_gate.shape[1]
    tm = min(ROW_TILE * 2, rows)
    return pl.pallas_call(
        _norm_proj_kernel,
        grid=(rows // tm, n // tn),
        in_specs=[pl.BlockSpec((tm, d), lambda i, j: (i, 0)),
                  pl.BlockSpec((1, d), lambda i, j: (0, 0)),
                  pl.BlockSpec((d, tn), lambda i, j: (0, j)),
                  pl.BlockSpec((d, ng), lambda i, j: (0, 0))],
        out_specs=[pl.BlockSpec((tm, tn), lambda i, j: (i, j)),
                   pl.BlockSpec((tm, ng), lambda i, j: (i, 0))],
        out_shape=[jax.ShapeDtypeStruct((rows, n), BF16), jax.ShapeDtypeStruct((rows, ng), F32)],
        scratch_shapes=[pltpu.VMEM((tm, d), BF16)],
        compiler_params=_params("parallel", "arbitrary"),
        name="norm_proj",
    )(h, gain.reshape(1, d), w, w_gate)


def _cross_attention(x, gain, wq_ref, kv_ref, wo_ref):
    hn = _rms(x, gain).astype(BF16)
    q = _dot(hn, wq_ref[...]).astype(BF16)
    scale = XATTN_HEAD_DIM ** -0.5
    outs = []
    for hd in range(XATTN_HEADS):
        lo = hd * XATTN_HEAD_DIM
        qh = q[:, lo:lo + XATTN_HEAD_DIM]
        kh = kv_ref[:, lo:lo + XATTN_HEAD_DIM]
        vh = kv_ref[:, XATTN_WIDTH + lo:XATTN_WIDTH + lo + XATTN_HEAD_DIM]
        s = _dot_nt(qh, kh) * scale
        p = jnp.exp(s - jnp.max(s, axis=-1, keepdims=True))
        denom = jnp.sum(p, axis=-1, keepdims=True)
        outs.append((_dot(p.astype(BF16), vh) / denom).astype(BF16))
    return _dot(jnp.concatenate(outs, axis=-1), wo_ref[...])


def _post_mixer_kernel(*refs, n_mix, final_norm):
    mix_refs, wout_refs = refs[:n_mix], refs[n_mix:2 * n_mix]
    (h_ref, gx_ref, wq_ref, kv_ref, wo_ref, gm_ref, wu_ref, wd_ref, gf_ref,
     o_ref, hn_ref) = refs[2 * n_mix:]
    f = pl.program_id(1)

    @pl.when(f == 0)
    def _():
        x = h_ref[...]
        for m_ref, w_ref in zip(mix_refs, wout_refs):
            x = x + _dot(m_ref[...], w_ref[...])
        x = x + _cross_attention(x, gx_ref[...], wq_ref, kv_ref, wo_ref)
        hn_ref[...] = _rms(x, gm_ref[...]).astype(BF16)
        o_ref[...] = x

    u = _dot(hn_ref[...], wu_ref[...])
    u = jnp.square(jnp.maximum(u, 0.0)).astype(BF16)
    o_ref[...] += _dot(u, wd_ref[...])

    if final_norm:
        @pl.when(f == pl.num_programs(1) - 1)
        def _():
            o_ref[...] = _rms(o_ref[...], gf_ref[...])


def post_mixer(h, mixes, w_outs, gain_x, wq, kv, wo, gain_mlp, w_up, w_down, final_gain, final_norm, batch):
    rows, d = h.shape
    seq = rows // batch
    mem_len = kv.shape[0] // batch
    ff = w_up.shape[1]
    tm = min(MLP_ROW_TILE, seq)
    tf = min(MLP_FF_TILE, ff)
    tiles_per_batch = seq // tm

    def const(shape):
        return pl.BlockSpec(shape, lambda i, f: (0, 0))

    in_specs = [pl.BlockSpec((tm, m.shape[1]), lambda i, f: (i, 0)) for m in mixes]
    in_specs += [const(w.shape) for w in w_outs]
    in_specs += [pl.BlockSpec((tm, d), lambda i, f: (i, 0)), const((1, d)), const(wq.shape),
                 pl.BlockSpec((mem_len, 2 * XATTN_WIDTH), lambda i, f: (i // tiles_per_batch, 0)),
                 const(wo.shape), const((1, d)),
                 pl.BlockSpec((d, tf), lambda i, f: (0, f)),
                 pl.BlockSpec((tf, d), lambda i, f: (f, 0)),
                 const((1, d))]
    return pl.pallas_call(
        functools.partial(_post_mixer_kernel, n_mix=len(mixes), final_norm=final_norm),
        grid=(rows // tm, ff // tf),
        in_specs=in_specs,
        out_specs=pl.BlockSpec((tm, d), lambda i, f: (i, 0)),
        out_shape=jax.ShapeDtypeStruct((rows, d), F32),
        scratch_shapes=[pltpu.VMEM((tm, d), BF16)],
        compiler_params=_params("parallel", "arbitrary"),
        name="post_mixer",
    )(*mixes, *w_outs, h, gain_x.reshape(1, d), wq, kv, wo, gain_mlp.reshape(1, d), w_up, w_down,
      final_gain.reshape(1, d))


def _fox_gate_kernel(fg_ref, bias_ref, ccol_ref, crow_ref):
    seq = fg_ref.shape[0]
    blk = FOX_TILE
    carry = jnp.zeros((1, LANES), F32)
    for i in range(seq // blk):
        rows = slice(i * blk, (i + 1) * blk)
        c = _cumsum_rows(_log_sigmoid(fg_ref[rows, :] + bias_ref[...])) + carry
        c2 = c * LOG2E
        ccol_ref[rows, :] = c2
        crow_ref[0, :, rows] = c2.T[:FOX_HEADS, :]
        carry = c[blk - 1:blk, :]


def fox_gates(gates, bias, batch, col_block):
    rows = gates.shape[0]
    seq = rows // batch
    return pl.pallas_call(
        _fox_gate_kernel,
        grid=(batch,),
        in_specs=[pl.BlockSpec((seq, LANES), lambda b: (b, col_block)),
                  pl.BlockSpec((1, LANES), lambda b: (0, 0))],
        out_specs=[pl.BlockSpec((seq, LANES), lambda b: (b, 0)),
                   pl.BlockSpec((1, FOX_HEADS, seq), lambda b: (b, 0, 0))],
        out_shape=[jax.ShapeDtypeStruct((rows, LANES), F32),
                   jax.ShapeDtypeStruct((batch, FOX_HEADS, seq), F32)],
        compiler_params=_params("parallel"),
        name="fox_gates",
    )(gates, bias)


FOX_ONES_ROWS = 16


def _fox_kernel(q_ref, k_ref, v_ref, ccol_ref, crow_ref, o_ref, vt_ref, cs_ref, st_ref, p_ref, alpha_ref,
                acc_ref, m_ref):
    pair = pl.program_id(1)
    tq, tk = FOX_TQ, FOX_TK
    seq = k_ref.shape[0]
    hd = FOX_HEAD_DIM

    lane = lax.broadcasted_iota(jnp.int32, (tk, LANES), 1)
    for blk in range(seq // tk):
        rows = slice(blk * tk, (blk + 1) * tk)
        vt = v_ref[rows, :].astype(F32).T
        c = ccol_ref[rows, :]
        for hh in range(2):
            vt_ref[hh, 0:hd, rows] = vt[hh * hd:(hh + 1) * hd, :].astype(BF16)
            vt_ref[hh, hd:hd + FOX_ONES_ROWS, rows] = jnp.ones((FOX_ONES_ROWS, tk), BF16)
            col = jnp.sum(jnp.where(lane == 2 * pair + hh, c, 0.0), axis=-1, keepdims=True)
            cs_ref[hh, rows, :] = jnp.broadcast_to(col, (tk, LANES))

    row = lax.broadcasted_iota(jnp.int32, (LANES, tq), 0)
    key_minus_query = (lax.broadcasted_iota(jnp.int32, (tk, tq), 0)
                       - lax.broadcasted_iota(jnp.int32, (tk, tq), 1))

    def query_tile(qi):
        slot = qi % 2
        q_start = qi * tq
        n_kv = -(-(q_start + tq) // tk)
        qt = q_ref[q_start:q_start + tq, :].astype(F32).T
        qts = (jnp.where(row < hd, qt, 0.0).astype(BF16), jnp.where(row < hd, 0.0, qt).astype(BF16))
        cts = [crow_ref[0, pl.ds(2 * pair + hh, 1), q_start:q_start + tq] for hh in range(2)]

        def scores(j):
            k = k_ref[j * tk:(j + 1) * tk, :]
            return [_dot(k, qts[hh]) for hh in range(2)]

        def values_times_probs(j):
            return [_dot(vt_ref[hh, :, j * tk:(j + 1) * tk], p_ref[slot, j % 2, hh]) for hh in range(2)]

        first_scores = scores(0)
        for hh in range(2):
            st_ref[slot, 0, hh] = first_scores[hh]
            m_ref[slot, hh] = jnp.full(m_ref.shape[2:], MASK_VALUE, F32)

        for j in range(n_kv):
            par, prev = j % 2, 1 - j % 2
            if j > 0:
                pv = values_times_probs(j - 1)
            if j + 1 < n_kv:
                next_scores = scores(j + 1)
            for hh in range(2):
                cs = cs_ref[hh, j * tk:(j + 1) * tk, :]
                st = st_ref[slot, par, hh] + (cts[hh] - jnp.concatenate([cs] * (tq // LANES), axis=1))
                if (j + 1) * tk - 1 > q_start:
                    st = jnp.where(key_minus_query <= q_start - j * tk, st, MASK_VALUE)
                m_old = m_ref[slot, hh]
                m_new = jnp.maximum(m_old, jnp.max(st, axis=0, keepdims=True))
                p_ref[slot, par, hh] = jnp.exp2(st - m_new).astype(BF16)
                if j == 1:
                    acc_ref[slot, hh] = pv[hh]
                elif j > 1:
                    acc_ref[slot, hh] = alpha_ref[slot, prev, hh] * acc_ref[slot, hh] + pv[hh]
                alpha_ref[slot, par, hh] = jnp.exp2(m_old - m_new)
                m_ref[slot, hh] = m_new
            if j + 1 < n_kv:
                for hh in range(2):
                    st_ref[slot, prev, hh] = next_scores[hh]

        last = n_kv - 1
        pv = values_times_probs(last)
        outs = []
        for hh in range(2):
            acc = pv[hh] if last == 0 else alpha_ref[slot, last % 2, hh] * acc_ref[slot, hh] + pv[hh]
            outs.append(acc[0:hd, :] / acc[hd:hd + 1, :])
        o_ref[q_start:q_start + tq, :] = jnp.concatenate(outs, axis=0).T.astype(o_ref.dtype)

    for qi in range(seq // tq):
        query_tile(qi)


def fox_attention(proj, ccol, crow, batch):
    rows = proj.shape[0]
    seq = rows // batch
    tq, tk = FOX_TQ, FOX_TK
    acc_rows = FOX_HEAD_DIM + FOX_ONES_ROWS
    return pl.pallas_call(
        _fox_kernel,
        grid=(batch, FOX_PAIRS),
        in_specs=[pl.BlockSpec((seq, LANES), lambda b, p: (b, p)),
                  pl.BlockSpec((seq, LANES), lambda b, p: (b, FOX_PAIRS + p)),
                  pl.BlockSpec((seq, LANES), lambda b, p: (b, 2 * FOX_PAIRS + p)),
                  pl.BlockSpec((seq, LANES), lambda b, p: (b, 0)),
                  pl.BlockSpec((1, FOX_HEADS, seq), lambda b, p: (b, 0, 0))],
        out_specs=pl.BlockSpec((seq, LANES), lambda b, p: (b, p)),
        out_shape=jax.ShapeDtypeStruct((rows, FOX_WIDTH), BF16),
        scratch_shapes=[pltpu.VMEM((2, acc_rows, seq), BF16), pltpu.VMEM((2, seq, LANES), F32),
                        pltpu.VMEM((2, 2, 2, tk, tq), F32), pltpu.VMEM((2, 2, 2, tk, tq), BF16),
                        pltpu.VMEM((2, 2, 2, 1, tq), F32),
                        pltpu.VMEM((2, 2, acc_rows, tq), F32), pltpu.VMEM((2, 2, 1, tq), F32)],
        compiler_params=_params("parallel", "parallel"),
        name="fox_attention",
    )(proj, proj, proj, ccol, crow)


def _hgrn_kernel(hq_ref, hi_ref, hg_ref, hf_ref, lbl_ref, gain_ref, o_ref, state_ref, *, layer):
    @pl.when(pl.program_id(2) == 0)
    def _():
        state_ref[...] = jnp.zeros(state_ref.shape, F32)

    logits = lbl_ref[...]
    w = jnp.exp(logits - jnp.max(logits, axis=0, keepdims=True))
    w = w / jnp.sum(w, axis=0, keepdims=True)
    lb = jnp.sum(w[:layer + 1], axis=0, keepdims=True) - w[0:1]
    log_lb = jnp.log(jnp.maximum(lb, LOG_FLOOR))
    log1m_lb = jnp.log1p(-lb)

    ch, sub = HGRN_CHUNK, HGRN_SUB
    n_sub = ch // sub
    lane_c = lax.broadcasted_iota(jnp.int32, (sub, ch), 1)
    row_c = lax.broadcasted_iota(jnp.int32, (sub, ch), 0)

    def chunk(ci):
        rows = slice(ci * ch, (ci + 1) * ch)
        hf = hf_ref[rows, :]
        hq = hq_ref[rows, :].astype(F32)
        v = hi_ref[rows, :].astype(F32)
        hg = hg_ref[rows, :].astype(F32)

        log_gate = log1m_lb + _log_sigmoid(hf)
        log_f = jnp.maximum(log_lb, log_gate) + _log1p_exp_neg(jnp.abs(log_lb - log_gate))
        k = (1.0 - lb) * _sigmoid(-hf)
        q = hq * _sigmoid(hq)
        b = _cumsum_rows(log_f) * LOG2E

        bref = jnp.concatenate(
            [jnp.broadcast_to(b[max(i * sub - 1, 0):max(i * sub - 1, 0) + 1, :], (sub, HGRN_DIM))
             for i in range(n_sub)], axis=0)
        q_rel = (q * jnp.exp2(b - bref)).astype(BF16)

        a_rows = []
        for i in range(n_sub):
            s0 = i * sub
            bq = b[s0:s0 + sub, :]
            qq = q[s0:s0 + sub, :]
            diag = jnp.zeros((sub, ch), F32)
            for s in range(sub):
                bs = b[s0 + s:s0 + s + 1, :]
                ks = k[s0 + s:s0 + s + 1, :]
                e = jnp.exp2(bq - bs)
                col = jnp.sum(qq * ks * e, axis=-1, keepdims=True)
                diag = jnp.where(lane_c == s0 + s, col, diag)
            diag = jnp.where(row_c + s0 >= lane_c, diag, 0.0)
            if i == 0:
                a_rows.append(diag)
            else:
                k_rel = (k * jnp.exp2(jnp.minimum(bref[s0:s0 + 1, :] - b, 0.0))).astype(BF16)
                off = _dot_nt(q_rel[s0:s0 + sub, :], k_rel)
                a_rows.append(jnp.where(lane_c < s0, off, diag))
        a = jnp.concatenate(a_rows, axis=0).astype(BF16)

        state_t = state_ref[...]
        b_last = b[ch - 1:ch, :]
        out = _dot(a, v.astype(BF16)) + _dot_nt((q * jnp.exp2(b)).astype(BF16), state_t.astype(BF16))
        k_end = (k * jnp.exp2(b_last - b)).astype(BF16)
        state_ref[...] = state_t * jnp.exp2(b_last) + _dot(v.T.astype(BF16), k_end)

        y = out * lax.rsqrt(jnp.mean(out * out, axis=-1, keepdims=True) + NORM_EPS) * gain_ref[...]
        o_ref[rows, :] = (y * (hg * _sigmoid(hg))).astype(o_ref.dtype)

    for ci in range(o_ref.shape[0] // ch):
        chunk(ci)


def hgrn2(proj, gates, lb_logits, gain, batch, layer, col_q, col_i, col_g):
    rows = proj.shape[0]
    seq = rows // batch
    tr = min(HGRN_ROWS, seq)
    nt = seq // tr
    n_layers = lb_logits.shape[0]

    def tok(col0):
        return pl.BlockSpec((tr, LANES), lambda b, h, t: (b * nt + t, col0 + h))

    return pl.pallas_call(
        functools.partial(_hgrn_kernel, layer=layer),
        grid=(batch, HGRN_HEADS, nt),
        in_specs=[tok(col_q), tok(col_i), tok(col_g), tok(0),
                  pl.BlockSpec((n_layers, LANES), lambda b, h, t: (0, h)),
                  pl.BlockSpec((1, LANES), lambda b, h, t: (0, h))],
        out_specs=tok(0),
        out_shape=jax.ShapeDtypeStruct((rows, HGRN_WIDTH), BF16),
        scratch_shapes=[pltpu.VMEM((HGRN_DIM, HGRN_DIM), F32)],
        compiler_params=_params("parallel", "parallel", "arbitrary"),
        name="hgrn2",
    )(proj, proj, proj, gates, lb_logits, gain.reshape(1, HGRN_WIDTH))


def _causal_conv_silu(x, tail, w, bias):
    xx = jnp.concatenate([tail, x], axis=0)
    acc = xx * w[0:1, :]
    for j in range(1, SSM_CONV):
        acc = pltpu.roll(acc, 1, axis=0) + xx * w[j:j + 1, :]
    acc = acc[tail.shape[0]:, :] + bias
    return acc * _sigmoid(acc)


def _pair_select(first, a, b):
    return jnp.where(first, a, b)


def _ssd_kernel(z_ref, x_ref, bc_ref, dt_ref, cwx_ref, cwbc_ref, cbx_ref, cbbc_ref, dtb_ref, alog_ref,
                dskip_ref, gain_ref, o_ref, state_ref, tailx_ref, tailbc_ref):
    @pl.when(pl.program_id(1) == 0)
    def _():
        state_ref[...] = jnp.zeros(state_ref.shape, F32)
        tailx_ref[...] = jnp.zeros(tailx_ref.shape, F32)
        tailbc_ref[...] = jnp.zeros(tailbc_ref.shape, F32)

    ch = SSM_CHUNK
    a2 = -jnp.exp(alog_ref[...]) * LOG2E
    r = lax.broadcasted_iota(jnp.int32, (ch, ch), 0)
    c = lax.broadcasted_iota(jnp.int32, (ch, ch), 1)
    causal = r >= c
    first = lax.broadcasted_iota(jnp.int32, (ch, LANES), 1) < SSM_HEAD_DIM
    first_row = first[0:1, :]

    def bcast(col):
        return jnp.broadcast_to(col, (ch, LANES))

    def conv(ref, tail_ref, cw_ref, cb_ref, rows, cols):
        raw = ref[rows, cols].astype(F32)
        out = _causal_conv_silu(raw, tail_ref[:, cols], cw_ref[:, cols], cb_ref[:, cols])
        tail_ref[:, cols] = raw[ch - 8:, :]
        return out

    def chunk(rows):
        dt = _softplus(dt_ref[rows, :] + dtb_ref[...])
        cum = _cumsum_rows(dt * a2)
        cum_t = cum.T
        dt_t = dt.T
        w_end =jnp.exp2(cum[ch - 1:ch, :] - cum) * dt

        for g in range(SSM_GROUPS):
            gx = slice(g * SSM_GROUP_WIDTH, (g + 1) * SSM_GROUP_WIDTH)
            gb = slice(g * SSM_STATE, (g + 1) * SSM_STATE)
            gc = slice(SSM_BC + g * SSM_STATE, SSM_BC + (g + 1) * SSM_STATE)
            xs = conv(x_ref, tailx_ref, cwx_ref, cbx_ref, rows, gx)
            bm = conv(bc_ref, tailbc_ref, cwbc_ref, cbbc_ref, rows, gb)
            cm = conv(bc_ref, tailbc_ref, cwbc_ref, cbbc_ref, rows, gc)
            bm16, cm16 = bm.astype(BF16), cm.astype(BF16)
            cb = _dot_nt(cm16, bm16)
            state = state_ref[g]
            y_state = _dot(cm16, state.astype(BF16))

            y_pairs, wx_pairs, decay_pairs = [], [], []
            for pr in range(2):
                xp = xs[:, pr * LANES:(pr + 1) * LANES]
                xp16 = xp.astype(BF16)
                ys, scale_in, scale_w, ends = [], [], [], []
                for hh in range(2):
                    h = g * 4 + pr * 2 + hh
                    ccol = bcast(cum[:, h:h + 1])
                    seg = jnp.where(causal, ccol - cum_t[h:h + 1, :], MASK_VALUE)
                    m = (cb * jnp.exp2(seg) * dt_t[h:h + 1, :]).astype(BF16)
                    ys.append(_dot(m, xp16))
                    scale_in.append(jnp.exp2(ccol))
                    scale_w.append(bcast(w_end[:, h:h + 1]))
                    ends.append(ccol[ch - 1:ch, :])
                y_in = y_state[:, pr * LANES:(pr + 1) * LANES] * _pair_select(first, scale_in[0], scale_in[1])
                y_pairs.append(_pair_select(first, ys[0], ys[1]) + y_in)
                wx_pairs.append((xp * _pair_select(first, scale_w[0], scale_w[1])).astype(BF16))
                decay_pairs.append(jnp.exp2(_pair_select(first_row, ends[0], ends[1])))
            y = jnp.concatenate(y_pairs, axis=-1)
            wx = jnp.concatenate(wx_pairs, axis=-1)
            state_ref[g] = state * jnp.concatenate(decay_pairs, axis=-1) + _dot(bm.T.astype(BF16), wx)

            y = y + dskip_ref[:, gx] * xs
            zg = z_ref[rows, gx].astype(F32)
            y = y * (zg * _sigmoid(zg))
            y = y * lax.rsqrt(jnp.mean(y * y, axis=-1, keepdims=True) + NORM_EPS) * gain_ref[:, gx]
            o_ref[rows, gx] = y.astype(o_ref.dtype)

    for ci in range(o_ref.shape[0] // ch):
        chunk(slice(ci * ch, (ci + 1) * ch))


def ssd_mixer(proj, dt_raw, conv_w, conv_b, dt_bias, a_log, d_skip, gain, batch):
    rows = proj.shape[0]
    seq = rows // batch
    tr = min(SSM_ROWS, seq)
    nt = seq // tr

    def tok(width, col):
        return pl.BlockSpec((tr, width), lambda b, c: (b * nt + c, col))

    def par(nrows, width, col):
        return pl.BlockSpec((nrows, width), lambda b, c: (0, col))

    return pl.pallas_call(
        _ssd_kernel,
        grid=(batch, nt),
        in_specs=[tok(SSM_INNER, 0), tok(SSM_INNER, 1), tok(2 * SSM_BC, 2), tok(LANES, 0),
                  par(SSM_CONV, SSM_INNER, 0), par(SSM_CONV, 2 * SSM_BC, 1),
                  par(1, SSM_INNER, 0), par(1, 2 * SSM_BC, 1),
                  par(1, LANES, 0), par(1, LANES, 0), par(1, SSM_INNER, 0), par(1, SSM_INNER, 0)],
        out_specs=tok(SSM_INNER, 0),
        out_shape=jax.ShapeDtypeStruct((rows, SSM_INNER), BF16),
        scratch_shapes=[pltpu.VMEM((SSM_GROUPS, SSM_STATE, SSM_GROUP_WIDTH), F32),
                        pltpu.VMEM((8, SSM_INNER), F32), pltpu.VMEM((8, 2 * SSM_BC), F32)],
        compiler_params=_params("parallel", "arbitrary"),
        name="ssd_mixer",
    )(proj, proj, proj, dt_raw, conv_w, conv_w, conv_b, conv_b, dt_bias, a_log, d_skip, gain)


def _pad_lanes(x):
    return jnp.pad(x, ((0, 0), (0, LANES - x.shape[1])))


def _split_cols(w, sizes):
    out, lo = [], 0
    for s in sizes:
        out.append(w[:, lo:lo + s])
        lo += s
    return out


def kernel(x, mem, mem_norm, norm_mix, norm_xattn, norm_mlp, norm_final, ev_in_proj, fox_fgate_bias,
           hgrn_lb_logits, hgrn_out_norm, ev_out_proj, ssm_in_proj, ssm_conv_w, ssm_conv_b, ssm_dt_bias,
           ssm_A_log, ssm_D, ssm_norm, ssm_out_proj, xa_q, xa_kv, xa_o, mlp_up, mlp_down):
    batch, seq, d = x.shape
    depth = norm_mix.shape[0]
    h = x.reshape(batch * seq, d)
    mem2 = mem.reshape(-1, d)

    for layer in range(depth):
        if layer % 2 == 0:
            e = layer // 2
            fq, fk, fv, fg, hq, hf, hi, hg = _split_cols(
                ev_in_proj[e], (FOX_WIDTH, FOX_WIDTH, FOX_WIDTH, FOX_HEADS,
                                HGRN_WIDTH, HGRN_WIDTH, HGRN_WIDTH, HGRN_WIDTH))
            w_main = jnp.concatenate([fq * (FOX_HEAD_DIM ** -0.5 * LOG2E), fk, fv, hq, hi, hg], axis=1).astype(BF16)
            w_gate = jnp.concatenate([hf, _pad_lanes(fg)], axis=1).astype(BF16)
            proj, gates = norm_proj(h, norm_mix[layer], w_main, w_gate, EVEN_PROJ_TILE)
            ccol, crow = fox_gates(gates, _pad_lanes(fox_fgate_bias[e].reshape(1, -1)), batch,
                                   HGRN_WIDTH // LANES)
            a_out = fox_attention(proj, ccol, crow, batch)
            blocks = FOX_WIDTH // LANES
            b_out = hgrn2(proj, gates, hgrn_lb_logits, hgrn_out_norm[e], batch, e,
                          3 * blocks, 3 * blocks + HGRN_HEADS, 3 * blocks + 2 * HGRN_HEADS)
            w_out = ev_out_proj[e].astype(BF16)
            mixes, w_outs = [a_out, b_out], [w_out[:FOX_WIDTH], w_out[FOX_WIDTH:]]
        else:
            o = layer // 2
            w_in = ssm_in_proj[o]
            n_main = 2 * SSM_INNER + 2 * SSM_BC
            proj, dt_raw = norm_proj(h, norm_mix[layer], w_in[:, :n_main].astype(BF16),
                                     _pad_lanes(w_in[:, n_main:]).astype(BF16), SSM_PROJ_TILE)
            y = ssd_mixer(proj, dt_raw, ssm_conv_w[o], ssm_conv_b[o].reshape(1, -1),
                          _pad_lanes(ssm_dt_bias[o].reshape(1, -1)), _pad_lanes(ssm_A_log[o].reshape(1, -1)),
                          jnp.repeat(ssm_D[o], SSM_HEAD_DIM).reshape(1, -1), ssm_norm[o].reshape(1, -1), batch)
            mixes, w_outs = [y], [ssm_out_proj[o].astype(BF16)]

        kv = norm_matmul(mem2, mem_norm, xa_kv[layer].astype(BF16), BF16, 2 * XATTN_WIDTH)
        h = post_mixer(h, mixes, w_outs, norm_xattn[layer], xa_q[layer].astype(BF16), kv,
                       xa_o[layer].astype(BF16), norm_mlp[layer], mlp_up[layer].astype(BF16),
                       mlp_down[layer].astype(BF16), norm_final, layer == depth - 1, batch)

    return h.reshape(batch, seq, d)
```
